```python
import jax
import jax.numpy as jnp
from jax import lax
import numpy as np

D_MODEL = 1024
BATCH = 32
SEQ = 2048
DEPTH = 2

GRID_W = 64
CTX_LEN = 256
D_MIX = D_MODEL
D_MLSTM = D_MIX // 2
N_HEADS = 4
HEAD_DIM = D_MLSTM // N_HEADS
D_CONV = D_MIX - D_MLSTM
CONV_WIDTH = 31
D_FF = 2816
CHUNK = 64
N_MOD = 9
EPS = 1e-6
FGATE_BIAS_LO = 3.0
FGATE_BIAS_HI = 6.0
GATE_W_SCALE = 0.1

Q0 = 0
K0 = Q0 + D_MLSTM
V0 = K0 + D_MLSTM
O0 = V0 + D_MLSTM
G0 = O0 + D_MLSTM
U0 = G0 + 4 * N_HEADS
D_IN = U0 + 2 * D_CONV

kernel_name = 'hybrid_mlstm_conformer_macaron_dit'


def rmsnorm(x, g):
    xf = x.astype(jnp.float32)
    y = xf * lax.rsqrt(jnp.mean(jnp.square(xf), axis=-1, keepdims=True) + EPS)
    return (y * g.astype(jnp.float32)).astype(x.dtype)


def layernorm(x, g, b):
    xf = x.astype(jnp.float32)
    mu = jnp.mean(xf, axis=-1, keepdims=True)
    var = jnp.mean(jnp.square(xf - mu), axis=-1, keepdims=True)
    y = (xf - mu) * lax.rsqrt(var + EPS)
    return (y * g.astype(jnp.float32) + b.astype(jnp.float32)).astype(x.dtype)


def modulate(h, shift, scale):
    return h * (1.0 + scale) + shift


def swiglu(h, w_up, w_down):
    g, u = jnp.split(h @ w_up, 2, axis=-1)
    return (jax.nn.silu(g) * u) @ w_down


def mixer_project(u, w_in, b_in):
    p = u @ w_in + b_in
    B_, S_, _ = p.shape

    def heads(a):
        return a.reshape(B_, S_, N_HEADS, HEAD_DIM).transpose(0, 2, 1, 3).astype(jnp.float32)

    q = heads(p[..., Q0:K0]) * (HEAD_DIM ** -0.5)
    k = heads(p[..., K0:V0])
    v = heads(p[..., V0:O0])
    o = p[..., O0:G0]
    gates = p[..., G0:U0].astype(jnp.float32).reshape(B_, S_, 4, N_HEADS).transpose(2, 0, 3, 1)
    glu = p[..., U0:]
    return q, k, v, o, gates, glu


def init_state(b):
    return (jnp.zeros((b, N_HEADS, HEAD_DIM, HEAD_DIM), jnp.float32),
            jnp.zeros((b, N_HEADS, HEAD_DIM), jnp.float32),
            jnp.zeros((b, N_HEADS), jnp.float32))


def mlstm_scan(q, k, v, log_i, log_f, state):
    B_, H_, S_, Dh = q.shape
    nc = S_ // CHUNK

    def to_chunks(a):
        return jnp.moveaxis(a.reshape((B_, H_, nc, CHUNK) + a.shape[3:]), 2, 0)

    xs = (to_chunks(q), to_chunks(k), to_chunks(v), to_chunks(log_i), to_chunks(log_f))
    lower = jnp.tril(jnp.ones((CHUNK, CHUNK), dtype=bool))

    def step(carry, inp):
        C, n, m = carry
        qc, kc, vc, li, lf = inp
        b = jnp.cumsum(lf, axis=-1)
        dmat = b[..., :, None] - b[..., None, :] + li[..., None, :]
        dmat = jnp.where(lower, dmat, -jnp.inf)
        a = b + m[..., None]
        m_t = jnp.maximum(a, jnp.max(dmat, axis=-1))
        w = jnp.exp(dmat - m_t[..., None])
        inter = jnp.exp(a - m_t)
        s = jnp.einsum('bhtd,bhsd->bhts', qc, kc) * w
        num = jnp.einsum('bhts,bhse->bhte', s, vc) + inter[..., None] * jnp.einsum('bhtd,bhde->bhte', qc, C)
        den = jnp.sum(s, axis=-1) + inter * jnp.einsum('bhtd,bhd->bht', qc, n)
        den = jnp.maximum(jnp.abs(den), jnp.exp(-m_t))
        h = num / den[..., None]
        b_last = b[..., -1]
        g = b_last[..., None] - b + li
        m_new = jnp.maximum(b_last + m, jnp.max(g, axis=-1))
        wk = jnp.exp(g - m_new[..., None])
        decay = jnp.exp(b_last + m - m_new)
        C_new = decay[..., None, None] * C + jnp.einsum('bhs,bhsd,bhse->bhde', wk, kc, vc)
        n_new = decay[..., None] * n + jnp.einsum('bhs,bhsd->bhd', wk, kc)
        return (C_new, n_new, m_new), h

    state, hs = lax.scan(step, state, xs)
    h = jnp.moveaxis(hs, 0, 2).reshape(B_, H_, S_, Dh)
    return h, state


def mlstm_bidir(q, k, v, gates, init_fwd, init_bwd):
    li_f = gates[0]
    lf_f = jax.nn.log_sigmoid(gates[1])
    li_b = gates[2]
    lf_b = jax.nn.log_sigmoid(gates[3])
    h_f, st_f = mlstm_scan(q, k, v, li_f, lf_f, init_fwd)

    def flip(a):
        return jnp.flip(a, axis=2)

    h_b, st_b = mlstm_scan(flip(q), flip(k), flip(v), flip(li_b), flip(lf_b), init_bwd)
    return h_f + flip(h_b), st_f, st_b


def mlstm_out(h, o, g):
    B_, H_, S_, Dh = h.shape
    hn = h * lax.rsqrt(jnp.mean(jnp.square(h), axis=-1, keepdims=True) + EPS)
    hn = hn.transpose(0, 2, 1, 3).reshape(B_, S_, H_ * Dh) * g.astype(jnp.float32)
    return hn.astype(o.dtype) * jax.nn.sigmoid(o)


def conv_latent(z, w, b):
    B_, S_, Dc = z.shape
    rows = S_ // GRID_W
    half = Dc // 2
    z = z.reshape(B_, rows, GRID_W, Dc)
    dn = ('NHWC', 'HWIO', 'NHWC')
    zh = lax.conv_general_dilated(z[..., :half], w[:, :half].reshape(1, CONV_WIDTH, 1, half),
                                  (1, 1), 'SAME', dimension_numbers=dn, feature_group_count=half)
    zv = lax.conv_general_dilated(z[..., half:], w[:, half:].reshape(CONV_WIDTH, 1, 1, Dc - half),
                                  (1, 1), 'SAME', dimension_numbers=dn, feature_group_count=Dc - half)
    return jnp.concatenate([zh, zv], axis=-1).reshape(B_, S_, Dc) + b


def conv_context(z, w, b):
    Dc = z.shape[-1]
    y = lax.conv_general_dilated(z, w.reshape(CONV_WIDTH, 1, Dc), (1,), 'SAME',
                                 dimension_numbers=('NWC', 'WIO', 'NWC'), feature_group_count=Dc)
    return y + b


def head_group_mix(h, o, glu, conv_fn, mlstm_g, conv_w, conv_b, cn_g, cn_b, w_out):
    y_m = mlstm_out(h, o, mlstm_g)
    a, gl = jnp.split(glu, 2, axis=-1)
    z = conv_fn(a * jax.nn.sigmoid(gl), conv_w, conv_b)
    y_c = jax.nn.silu(layernorm(z, cn_g, cn_b))
    return jnp.concatenate([y_m, y_c], axis=-1) @ w_out


def setup_inputs(seed: int = 0) -> dict:
    key = jax.random.key(seed)
    ks = jax.random.split(key, 22)

    def nrm(k, shape, s):
        return jax.random.normal(k, shape, jnp.float32) * s

    def gain(k, shape):
        return 1.0 + nrm(k, shape, 0.02)

    D = D_MODEL
    f_bias = jnp.linspace(FGATE_BIAS_LO, FGATE_BIAS_HI, N_HEADS, dtype=jnp.float32)
    w_in = nrm(ks[10], (DEPTH, D, D_IN), D ** -0.5)
    w_in = w_in.at[:, :, G0:U0].multiply(GATE_W_SCALE)
    b_in = nrm(ks[11], (DEPTH, D_IN), 0.02)
    b_in = b_in.at[:, G0 + N_HEADS:G0 + 2 * N_HEADS].add(f_bias)
    b_in = b_in.at[:, G0 + 3 * N_HEADS:G0 + 4 * N_HEADS].add(f_bias)
    return {
        'x': nrm(ks[0], (BATCH, SEQ, D), 1.0),
        'c': nrm(ks[1], (BATCH, D), 1.0),
        'ctx': nrm(ks[2], (BATCH, CTX_LEN, D), 1.0),
        'c_ctx': nrm(ks[3], (D,), 1.0),
        'w_mod': nrm(ks[4], (DEPTH, D, N_MOD * D), 0.5 * D ** -0.5),
        'b_mod': nrm(ks[5], (DEPTH, N_MOD * D), 0.02),
        'norm_ffn1': gain(ks[6], (DEPTH, D)),
        'ffn1_up': nrm(ks[7], (DEPTH, D, 2 * D_FF), D ** -0.5),
        'ffn1_down': nrm(ks[8], (DEPTH, D_FF, D), D_FF ** -0.5),
        'norm_mix': gain(ks[9], (DEPTH, D)),
        'w_in': w_in,
        'b_in': b_in,
        'mlstm_norm': gain(ks[12], (DEPTH, D_MLSTM)),
        'conv_w': nrm(ks[13], (DEPTH, CONV_WIDTH, D_CONV), CONV_WIDTH ** -0.5),
        'conv_b': nrm(ks[14], (DEPTH, D_CONV), 0.02),
        'conv_norm_g': gain(ks[15], (DEPTH, D_CONV)),
        'conv_norm_b': nrm(ks[16], (DEPTH, D_CONV), 0.02),
        'w_out': nrm(ks[17], (DEPTH, D_MIX, D), D_MIX ** -0.5),
        'norm_ffn2': gain(ks[18], (DEPTH, D)),
        'ffn2_up': nrm(ks[19], (DEPTH, D, 2 * D_FF), D ** -0.5),
        'ffn2_down': nrm(ks[20], (DEPTH, D_FF, D), D_FF ** -0.5),
        'norm_final': gain(ks[21], (D,)),
    }


def reference(x, c, ctx, c_ctx, w_mod, b_mod, norm_ffn1, ffn1_up, ffn1_down, norm_mix, w_in, b_in,
              mlstm_norm, conv_w, conv_b, conv_norm_g, conv_norm_b, w_out, norm_ffn2, ffn2_up,
              ffn2_down, norm_final):
    B_ = x.shape[0]
    D = x.shape[-1]
    xc = ctx
    sc = jax.nn.silu(c)
    scc = jax.nn.silu(c_ctx)
    for l in range(DEPTH):
        last = l == DEPTH - 1
        mod = (sc @ w_mod[l] + b_mod[l]).reshape(B_, 1, N_MOD, D)
        mod_c = (scc @ w_mod[l] + b_mod[l]).reshape(1, 1, N_MOD, D)
        m = [mod[:, :, j] for j in range(N_MOD)]
        mc = [mod_c[:, :, j] for j in range(N_MOD)]

        x = x + 0.5 * m[2] * swiglu(modulate(rmsnorm(x, norm_ffn1[l]), m[0], m[1]), ffn1_up[l], ffn1_down[l])
        xc = xc + 0.5 * mc[2] * swiglu(modulate(rmsnorm(xc, norm_ffn1[l]), mc[0], mc[1]), ffn1_up[l], ffn1_down[l])

        u = modulate(rmsnorm(x, norm_mix[l]), m[3], m[4])
        uc = modulate(rmsnorm(xc, norm_mix[l]), mc[3], mc[4])
        q, k, v, o, gates, glu = mixer_project(u, w_in[l], b_in[l])
        qc, kc, vc, oc, gatesc, gluc = mixer_project(uc, w_in[l], b_in[l])
        zero = init_state(B_)
        hc, st_f, st_b = mlstm_bidir(qc, kc, vc, gatesc, zero, zero)
        h, _, _ = mlstm_bidir(q, k, v, gates, st_f, st_b)
        y = head_group_mix(h, o, glu, conv_latent, mlstm_norm[l], conv_w[l], conv_b[l],
                           conv_norm_g[l], conv_norm_b[l], w_out[l])
        x = x + m[5] * y
        if not last:
            yc = head_group_mix(hc, oc, gluc, conv_context, mlstm_norm[l], conv_w[l], conv_b[l],
                                conv_norm_g[l], conv_norm_b[l], w_out[l])
            xc = xc + mc[5] * yc

        x = x + 0.5 * m[8] * swiglu(modulate(rmsnorm(x, norm_ffn2[l]), m[6], m[7]), ffn2_up[l], ffn2_down[l])
        if not last:
            xc = xc + 0.5 * mc[8] * swiglu(modulate(rmsnorm(xc, norm_ffn2[l]), mc[6], mc[7]), ffn2_up[l], ffn2_down[l])
    return rmsnorm(x, norm_final)
```

```python
import functools

import jax
import jax.numpy as jnp
from jax import lax
from jax.experimental import pallas as pl
from jax.experimental.pallas import tpu as pltpu

F32 = jnp.float32
BF16 = jnp.bfloat16

GRID_W = 64
N_HEADS = 4
HEAD_DIM = 128
D_MLSTM = N_HEADS * HEAD_DIM
D_CONV = 512
CONV_WIDTH = 31
CONV_PAD = (CONV_WIDTH - 1) // 2
N_MOD = 9
EPS = 1e-6

LANES = 128
FF_CHUNK = 256
MLSTM_CHUNK = 128
STATE_ROWS = HEAD_DIM + 16
NEG_BIG = -1e30
VMEM_LIMIT = 56 * 1024 * 1024


def _cparams(n_axes):
    return pltpu.CompilerParams(
        dimension_semantics=("arbitrary",) * n_axes, vmem_limit_bytes=VMEM_LIMIT)


def _resident(shape):
    nd = len(shape)
    return pl.BlockSpec(shape, lambda *_: (0,) * nd, pipeline_mode=pl.Buffered(1))


def _rms_mod(x, g, shift, scale):
    ms = jnp.mean(x * x, axis=-1, keepdims=True)
    return (x * lax.rsqrt(ms + EPS) * g) * (1.0 + scale) + shift


def _mod_kernel(c_ref, w_ref, b_ref, o_ref):
    c = c_ref[...]
    s = (c * jax.nn.sigmoid(c)).astype(BF16)
    o_ref[0] = jnp.dot(s, w_ref[0].astype(BF16), preferred_element_type=F32) + b_ref[0]


def _mod_call(c_all, w_mod, b_mod):
    depth, d, n = w_mod.shape
    r = c_all.shape[0]
    tn = n // 8
    return pl.pallas_call(
        _mod_kernel,
        grid=(depth, n // tn),
        in_specs=[pl.BlockSpec((r, d), lambda l, j: (0, 0)),
                  pl.BlockSpec((1, d, tn), lambda l, j: (l, 0, j)),
                  pl.BlockSpec((1, 1, tn), lambda l, j: (l, 0, j))],
        out_specs=pl.BlockSpec((1, r, tn), lambda l, j: (l, 0, j)),
        out_shape=jax.ShapeDtypeStruct((depth, r, n), F32),
        compiler_params=_cparams(2),
        name="mod",
    )(c_all, w_mod, b_mod.reshape(depth, 1, n))


def _ffn_kernel(x_ref, mod_ref, ng_ref, wg_ref, wu_ref, wd_ref, *rest, js, final):
    if final:
        gf_ref, o_ref, h_ref, acc_ref = rest
    else:
        o_ref, h_ref, acc_ref = rest
    x = x_ref[0]
    h_ref[...] = _rms_mod(x, ng_ref[...], mod_ref[0, js], mod_ref[0, js + 1]).astype(BF16)
    acc_ref[...] = jnp.zeros_like(acc_ref)

    def chunk(c, carry):
        h = h_ref[...]
        g = jnp.dot(h, wg_ref[c], preferred_element_type=F32)
        u = jnp.dot(h, wu_ref[c], preferred_element_type=F32)
        a = (g * jax.nn.sigmoid(g) * u).astype(BF16)
        acc_ref[...] += jnp.dot(a, wd_ref[c], preferred_element_type=F32)
        return carry

    lax.fori_loop(0, wg_ref.shape[0], chunk, 0)
    out = x + (0.5 * mod_ref[0, js + 2]) * acc_ref[...]
    if final:
        ms = jnp.mean(out * out, axis=-1, keepdims=True)
        out = out * lax.rsqrt(ms + EPS) * gf_ref[...]
    o_ref[0] = out


def _ffn_call(x, mods, ng, wg, wu, wd, *, js, mod_row, tm, final_g=None):
    b, t, d = x.shape
    nck, _, fc = wg.shape
    row = (lambda i, j: (i, 0, 0, 0)) if mod_row is None else (lambda i, j: (mod_row, 0, 0, 0))
    in_specs = [pl.BlockSpec((1, tm, d), lambda i, j: (i, j, 0)),
                pl.BlockSpec((1, N_MOD, 1, d), row),
                _resident((1, d)), _resident(wg.shape), _resident(wu.shape), _resident(wd.shape)]
    args = [x, mods, ng.reshape(1, d), wg, wu, wd]
    if final_g is not None:
        in_specs.append(_resident((1, d)))
        args.append(final_g.reshape(1, d))
    return pl.pallas_call(
        functools.partial(_ffn_kernel, js=js, final=final_g is not None),
        grid=(b, t // tm),
        in_specs=in_specs,
        out_specs=pl.BlockSpec((1, tm, d), lambda i, j: (i, j, 0)),
        out_shape=jax.ShapeDtypeStruct((b, t, d), F32),
        scratch_shapes=[pltpu.VMEM((tm, d), BF16), pltpu.VMEM((tm, d), F32)],
        compiler_params=_cparams(2),
        name="ffn",
    )(*args)


def _proj_kernel(x_ref, mod_ref, ng_ref, wn_ref, bn_ref, wt_ref, bt_ref,
                 k_ref, z_ref, g_ref, qt_ref, vt_ref, ot_ref, *, chunk):
    tm = x_ref.shape[1]
    u = _rms_mod(x_ref[0], ng_ref[...], mod_ref[0, 3], mod_ref[0, 4]).astype(BF16)
    pn = jnp.dot(u, wn_ref[...], preferred_element_type=F32) + bn_ref[...]
    for h in range(N_HEADS):
        k_ref[0, h] = pn[:, h * HEAD_DIM:(h + 1) * HEAD_DIM].astype(BF16)
    a = pn[:, D_MLSTM:D_MLSTM + D_CONV]
    gl = pn[:, D_MLSTM + D_CONV:D_MLSTM + 2 * D_CONV]
    z_ref[0] = a * jax.nn.sigmoid(gl)
    g0 = D_MLSTM + 2 * D_CONV
    g_ref[0] = pn[:, g0:g0 + 4 * N_HEADS]
    pt = lax.dot_general(wt_ref[...], u, (((1,), (1,)), ((), ())), preferred_element_type=F32)
    scale = HEAD_DIM ** -0.5
    for h in range(N_HEADS):
        for lb in range(tm // LANES):
            c, off = divmod(lb * LANES, chunk)
            cols = slice(lb * LANES, (lb + 1) * LANES)
            dst = (0, h, c, slice(None), slice(off, off + LANES))
            rq = slice(h * HEAD_DIM, (h + 1) * HEAD_DIM)
            rv = slice(D_MLSTM + h * HEAD_DIM, D_MLSTM + (h + 1) * HEAD_DIM)
            ro = slice(2 * D_MLSTM + h * HEAD_DIM, 2 * D_MLSTM + (h + 1) * HEAD_DIM)
            qt_ref[dst] = ((pt[rq, cols] + bt_ref[rq, :]) * scale).astype(BF16)
            vt_ref[dst] = (pt[rv, cols] + bt_ref[rv, :]).astype(BF16)
            ot_ref[dst] = pt[ro, cols] + bt_ref[ro, :]


def _proj_call(x, mods, ng, wn, bn, wt, bt, *, mod_row, tm, chunk):
    b, t, d = x.shape
    nch, cpt = t // chunk, tm // chunk
    row = (lambda i, j: (i, 0, 0, 0)) if mod_row is None else (lambda i, j: (mod_row, 0, 0, 0))
    tspec = pl.BlockSpec((1, N_HEADS, cpt, HEAD_DIM, chunk), lambda i, j: (i, 0, j, 0, 0))
    tshape = (b, N_HEADS, nch, HEAD_DIM, chunk)
    return pl.pallas_call(
        functools.partial(_proj_kernel, chunk=chunk),
        grid=(b, t // tm),
        in_specs=[pl.BlockSpec((1, tm, d), lambda i, j: (i, j, 0)),
                  pl.BlockSpec((1, N_MOD, 1, d), row),
                  _resident((1, d)), _resident(wn.shape), _resident(bn.shape),
                  _resident(wt.shape), _resident(bt.shape)],
        out_specs=[pl.BlockSpec((1, N_HEADS, tm, HEAD_DIM), lambda i, j: (i, 0, j, 0)),
                   pl.BlockSpec((1, tm, D_CONV), lambda i, j: (i, j, 0)),
                   pl.BlockSpec((1, tm, 4 * N_HEADS), lambda i, j: (i, j, 0)),
                   tspec, tspec, tspec],
        out_shape=[jax.ShapeDtypeStruct((b, N_HEADS, t, HEAD_DIM), BF16),
                   jax.ShapeDtypeStruct((b, t, D_CONV), F32),
                   jax.ShapeDtypeStruct((b, t, 4 * N_HEADS), F32),
                   jax.ShapeDtypeStruct(tshape, BF16),
                   jax.ShapeDtypeStruct(tshape, BF16),
                   jax.ShapeDtypeStruct(tshape, F32)],
        compiler_params=_cparams(2),
        name="proj",
    )(x, mods, ng.reshape(1, d), wn, bn, wt, bt)


def _scan_max(r, reverse):
    n = r.shape[-1]
    lane = lax.broadcasted_iota(jnp.int32, r.shape, r.ndim - 1)
    s = 1
    while s < n:
        if reverse:
            sh = jnp.where(lane < n - s, pltpu.roll(r, n - s, r.ndim - 1), NEG_BIG)
        else:
            sh = jnp.where(lane >= s, pltpu.roll(r, s, r.ndim - 1), NEG_BIG)
        r = jnp.maximum(r, sh)
        s *= 2
    return r


def _mlstm_kernel(kc_ref, qtc_ref, vtc_ref, otc_ref, kl_ref, qtl_ref, vtl_ref, otl_ref,
                  g_ref, gain_ref, yc_ref, yl_ref, gs_ref, rb_ref, hs_ref, st_ref, *, ctx_out):
    n_c, n_l = qtc_ref.shape[2], qtl_ref.shape[2]
    n_ch = n_c + n_l
    L = qtl_ref.shape[4]
    g = g_ref[0, 0]
    srow = lax.broadcasted_iota(jnp.int32, (L, L), 0)
    tcol = lax.broadcasted_iota(jnp.int32, (L, L), 1)
    masks = (srow <= tcol, srow >= tcol)
    orders = (list(range(n_ch)),
              list(range(n_c - 1, -1, -1)) + list(range(n_ch - 1, n_c - 1, -1)))

    for d in range(2):
        li, fpre = g[2 * d], g[2 * d + 1]
        lf = jnp.minimum(fpre, 0.0) - jnp.log1p(jnp.exp(-jnp.abs(fpre)))
        tri = masks[d].astype(F32)
        cum = jnp.dot(lf, tri, preferred_element_type=F32, precision=lax.Precision.HIGHEST)
        edge = L - 1 if d == 0 else 0
        total = cum[:, edge:edge + 1]
        r = li - cum
        pm = _scan_max(r, reverse=(d == 1))
        rmax = pm[:, edge:edge + 1]
        m = jnp.zeros((1, 1), F32)
        m_prev, m_new = [None] * n_ch, [None] * n_ch
        for c in orders[d]:
            m_prev[c] = m
            m = total[c:c + 1] + jnp.maximum(m, rmax[c:c + 1])
            m_new[c] = m
        m_prev = jnp.concatenate(m_prev, axis=0)
        m_new = jnp.concatenate(m_new, axis=0)
        mx = jnp.maximum(m_prev, pm)
        alpha = -mx
        inter = jnp.exp(m_prev - mx)
        clampv = jnp.exp(-(cum + mx))
        wk = jnp.exp(total + r - m_new)
        decay = jnp.broadcast_to(jnp.exp(total + m_prev - m_new), (n_ch, L))
        r_t = jnp.concatenate([r, jnp.zeros((L - n_ch, L), F32)], axis=0).T
        pad = jnp.zeros((3, L), F32)
        for c in range(n_ch):
            gs_ref[d, c] = jnp.concatenate(
                [alpha[c:c + 1], inter[c:c + 1], clampv[c:c + 1], wk[c:c + 1], decay[c:c + 1], pad],
                axis=0)
            rb_ref[d, c] = jnp.where(masks[d], jnp.broadcast_to(r_t[:, c:c + 1], (L, L)), NEG_BIG)

    st_ref[...] = jnp.zeros_like(st_ref)

    def step(d, cidx, k, qt, vt, need_h):
        gsl = gs_ref[d, cidx]
        alpha, inter, clampv, wk, decay = (gsl[i:i + 1] for i in range(5))
        state = st_ref[d]
        if need_h:
            res = jnp.dot(jnp.concatenate([k, state.astype(BF16)], axis=0), qt,
                          preferred_element_type=F32)
            s_t = res[:L]
            w = jnp.exp(rb_ref[d, cidx] + alpha)
            sw = s_t * w
            den = jnp.sum(sw, axis=0, keepdims=True) + inter * res[L + HEAD_DIM:L + HEAD_DIM + 1]
            num = (jnp.dot(vt, sw.astype(BF16), preferred_element_type=F32)
                   + inter * res[L:L + HEAD_DIM])
            den = jnp.maximum(jnp.abs(den), clampv)
            hs_ref[d, cidx] = num / den
        vs = vt.astype(F32) * wk
        lhs = jnp.concatenate(
            [vs, wk, jnp.zeros((STATE_ROWS - HEAD_DIM - 1, L), F32)], axis=0).astype(BF16)
        upd = jnp.dot(lhs, k, preferred_element_type=F32)
        st_ref[d] = decay[:, :HEAD_DIM] * state + upd

    for i in range(n_c):
        for d, c in ((0, i), (1, n_c - 1 - i)):
            step(d, c, kc_ref[0, 0, c * L:(c + 1) * L, :], qtc_ref[0, 0, c], vtc_ref[0, 0, c], ctx_out)

    def latent(i, carry):
        for d, c in ((0, i), (1, n_l - 1 - i)):
            rows = pl.ds(pl.multiple_of(c * L, L), L)
            step(d, n_c + c, kl_ref[0, 0, rows, :], qtl_ref[0, 0, c], vtl_ref[0, 0, c], True)
        return carry

    lax.fori_loop(0, n_l, latent, 0)

    gain = gain_ref[...]

    def finish(cidx, ot):
        hsum = hs_ref[0, cidx] + hs_ref[1, cidx]
        ms = jnp.mean(hsum * hsum, axis=0, keepdims=True)
        y = hsum * lax.rsqrt(ms + EPS) * gain * jax.nn.sigmoid(ot)
        return y.T.astype(BF16)

    if ctx_out:
        for c in range(n_c):
            yc_ref[0, c * L:(c + 1) * L, :] = finish(c, otc_ref[0, 0, c])
    else:
        yc_ref[...] = jnp.zeros_like(yc_ref)

    def fin_latent(c, carry):
        rows = pl.ds(pl.multiple_of(c * L, L), L)
        yl_ref[0, rows, :] = finish(n_c + c, otl_ref[0, 0, c])
        return carry

    lax.fori_loop(0, n_l, fin_latent, 0)


def _mlstm_call(kc, qtc, vtc, otc, kl, qtl, vtl, otl, gates, gain, *, ctx_out):
    b = kl.shape[0]
    tc, tl = kc.shape[2], kl.shape[2]
    n_c, n_l, L = qtc.shape[2], qtl.shape[2], qtl.shape[4]
    n_ch = n_c + n_l
    assert L == LANES and n_ch <= L

    def nat(t):
        return pl.BlockSpec((1, 1, t, HEAD_DIM), lambda i, h: (i, h, 0, 0))

    def tr(n):
        return pl.BlockSpec((1, 1, n, HEAD_DIM, L), lambda i, h: (i, h, 0, 0, 0))

    return pl.pallas_call(
        functools.partial(_mlstm_kernel, ctx_out=ctx_out),
        grid=(b, N_HEADS),
        in_specs=[nat(tc), tr(n_c), tr(n_c), tr(n_c), nat(tl), tr(n_l), tr(n_l), tr(n_l),
                  pl.BlockSpec((1, 1, 4, n_ch, L), lambda i, h: (i, h, 0, 0, 0)),
                  pl.BlockSpec((HEAD_DIM, LANES), lambda i, h: (h, 0))],
        out_specs=[pl.BlockSpec((1, tc, HEAD_DIM), lambda i, h: (i, 0, h)),
                   pl.BlockSpec((1, tl, HEAD_DIM), lambda i, h: (i, 0, h))],
        out_shape=[jax.ShapeDtypeStruct((b, tc, D_MLSTM), BF16),
                   jax.ShapeDtypeStruct((b, tl, D_MLSTM), BF16)],
        scratch_shapes=[pltpu.VMEM((2, n_ch, 8, L), F32),
                        pltpu.VMEM((2, n_ch, L, L), F32),
                        pltpu.VMEM((2, n_ch, HEAD_DIM, L), F32),
                        pltpu.VMEM((2, STATE_ROWS, HEAD_DIM), F32)],
        compiler_params=_cparams(2),
        name="mlstm",
    )(kc, qtc, vtc, otc, kl, qtl, vtl, otl, gates, gain)


SEG_LPAD = 16
ROW_TILE = 64


def _conv_seg_kernel(z_ref, w_ref, o_ref, zp_ref, *, seg):
    t = z_ref.shape[1]
    pitch = seg + 2 * SEG_LPAD
    zp_ref[...] = jnp.zeros_like(zp_ref)
    for s in range(t // seg):
        zp_ref[s * pitch + SEG_LPAD:s * pitch + SEG_LPAD + seg, :] = z_ref[0, s * seg:(s + 1) * seg, :]
    tiles = seg // ROW_TILE

    def tile(i, carry):
        s, q = i // tiles, i % tiles
        src = s * pitch + q * ROW_TILE + (SEG_LPAD - CONV_PAD)
        acc = jnp.zeros((ROW_TILE, LANES), F32)
        for j in range(CONV_WIDTH):
            acc = acc + w_ref[j:j + 1, :] * zp_ref[pl.ds(src + j, ROW_TILE), :]
        o_ref[0, pl.ds(pl.multiple_of(i * ROW_TILE, ROW_TILE), ROW_TILE), :] = acc
        return carry

    lax.fori_loop(0, t // ROW_TILE, tile, 0)


def _conv_stride_kernel(z_ref, w_ref, o_ref, zp_ref, *, stride):
    t = z_ref.shape[1]
    pad = CONV_PAD * stride
    zp_ref[0:pad, :] = jnp.zeros((pad, LANES), F32)
    zp_ref[pad + t:, :] = jnp.zeros((pad, LANES), F32)
    zp_ref[pad:pad + t, :] = z_ref[0]

    def tile(i, carry):
        base = pl.multiple_of(i * ROW_TILE, ROW_TILE)
        acc = jnp.zeros((ROW_TILE, LANES), F32)
        for j in range(CONV_WIDTH):
            acc = acc + w_ref[j:j + 1, :] * zp_ref[pl.ds(base + j * stride, ROW_TILE), :]
        o_ref[0, pl.ds(base, ROW_TILE), :] = acc
        return carry

    lax.fori_loop(0, t // ROW_TILE, tile, 0)


def _conv_call(z, w, *, ch0, nblk, seg=None, stride=None):
    b, t, _ = z.shape
    if seg is not None:
        body = functools.partial(_conv_seg_kernel, seg=seg)
        rows = (t // seg) * (seg + 2 * SEG_LPAD)
    else:
        body = functools.partial(_conv_stride_kernel, stride=stride)
        rows = t + 2 * CONV_PAD * stride
    return pl.pallas_call(
        body,
        grid=(b, nblk),
        in_specs=[pl.BlockSpec((1, t, LANES), lambda i, j: (i, 0, ch0 + j)),
                  pl.BlockSpec((CONV_WIDTH, LANES), lambda i, j: (0, ch0 + j))],
        out_specs=pl.BlockSpec((1, t, LANES), lambda i, j: (i, 0, j)),
        out_shape=jax.ShapeDtypeStruct((b, t, nblk * LANES), F32),
        scratch_shapes=[pltpu.VMEM((rows, LANES), F32)],
        compiler_params=_cparams(2),
        name="conv",
    )(z, w)


def _mixout_kernel(*refs, nz):
    x_ref, mod_ref, ym_ref = refs[:3]
    z_refs = refs[3:3 + nz]
    cb_ref, cg_ref, cnb_ref, wm_ref, wc_ref, o_ref = refs[3 + nz:]
    z = jnp.concatenate([r[0] for r in z_refs], axis=-1) if nz > 1 else z_refs[0][0]
    z = z + cb_ref[...]
    mu = jnp.mean(z, axis=-1, keepdims=True)
    zc = z - mu
    var = jnp.mean(zc * zc, axis=-1, keepdims=True)
    ln = zc * lax.rsqrt(var + EPS) * cg_ref[...] + cnb_ref[...]
    yc = (ln * jax.nn.sigmoid(ln)).astype(BF16)
    y = (jnp.dot(ym_ref[0], wm_ref[...], preferred_element_type=F32)
         + jnp.dot(yc, wc_ref[...], preferred_element_type=F32))
    o_ref[0] = x_ref[0] + mod_ref[0, 5] * y


def _mixout_call(x, mods, ym, zs, cb, cg, cnb, wm, wc, *, mod_row, tm):
    b, t, d = x.shape
    row = (lambda i, j: (i, 0, 0, 0)) if mod_row is None else (lambda i, j: (mod_row, 0, 0, 0))

    def tok(c):
        return pl.BlockSpec((1, tm, c), lambda i, j: (i, j, 0))

    return pl.pallas_call(
        functools.partial(_mixout_kernel, nz=len(zs)),
        grid=(b, t // tm),
        in_specs=[tok(d), pl.BlockSpec((1, N_MOD, 1, d), row), tok(D_MLSTM)]
                 + [tok(z.shape[2]) for z in zs]
                 + [_resident((1, D_CONV))] * 3 + [_resident(wm.shape), _resident(wc.shape)],
        out_specs=tok(d),
        out_shape=jax.ShapeDtypeStruct((b, t, d), F32),
        compiler_params=_cparams(2),
        name="mixout",
    )(x, mods, ym, *zs, cb.reshape(1, -1), cg.reshape(1, -1), cnb.reshape(1, -1), wm, wc)


def _chunked_cols(w, n):
    k = w.shape[0]
    return w.reshape(k, n, FF_CHUNK).transpose(1, 0, 2)


def _gate_rows(g, chunk):
    b, t, _ = g.shape
    return g.reshape(b, t // chunk, chunk, 4, N_HEADS).transpose(0, 4, 3, 1, 2)


def kernel(x, c, ctx, c_ctx, w_mod, b_mod, norm_ffn1, ffn1_up, ffn1_down, norm_mix, w_in, b_in,
           mlstm_norm, conv_w, conv_b, conv_norm_g, conv_norm_b, w_out, norm_ffn2, ffn2_up,
           ffn2_down, norm_final):
    b, s, d = x.shape
    depth = w_mod.shape[0]
    d_ff = ffn1_down.shape[1]
    nck = d_ff // FF_CHUNK
    tm_l, tm_c = 512, ctx.shape[1]
    L = MLSTM_CHUNK

    rows = -(-(b + 1) // 8) * 8
    c_all = jnp.concatenate([c, c_ctx[None, :], jnp.zeros((rows - b - 1, d), F32)], axis=0)
    mods_all = _mod_call(c_all, w_mod, b_mod).reshape(depth, rows, N_MOD, 1, d)

    q0, k0, v0, o0 = 0, D_MLSTM, 2 * D_MLSTM, 3 * D_MLSTM
    g0 = 4 * D_MLSTM
    u0 = g0 + 4 * N_HEADS
    n_nat = D_MLSTM + 2 * D_CONV + LANES

    xc = ctx
    for l in range(depth):
        last = l == depth - 1
        mods = mods_all[l]

        def ffn_weights(up, down):
            return (_chunked_cols(up[:, :d_ff], nck).astype(BF16),
                    _chunked_cols(up[:, d_ff:], nck).astype(BF16),
                    down.reshape(nck, FF_CHUNK, d).astype(BF16))

        w1 = ffn_weights(ffn1_up[l], ffn1_down[l])
        x = _ffn_call(x, mods, norm_ffn1[l], *w1, js=0, mod_row=None, tm=tm_l)
        xc = _ffn_call(xc, mods, norm_ffn1[l], *w1, js=0, mod_row=b, tm=tm_c)

        wi, bi = w_in[l], b_in[l]
        wn = jnp.concatenate([wi[:, k0:v0], wi[:, u0:], wi[:, g0:u0],
                              jnp.zeros((d, LANES - 4 * N_HEADS), F32)], axis=1).astype(BF16)
        bn = jnp.concatenate([bi[k0:v0], bi[u0:], bi[g0:u0],
                              jnp.zeros((LANES - 4 * N_HEADS,), F32)]).reshape(1, n_nat)
        wt = jnp.concatenate([wi[:, q0:k0], wi[:, v0:o0], wi[:, o0:g0]], axis=1).T.astype(BF16)
        bt = jnp.broadcast_to(
            jnp.concatenate([bi[q0:k0], bi[v0:o0], bi[o0:g0]])[:, None], (3 * D_MLSTM, LANES))
        kl, zl, gl, qtl, vtl, otl = _proj_call(x, mods, norm_mix[l], wn, bn, wt, bt,
                                               mod_row=None, tm=tm_l, chunk=L)
        kc, zc, gc, qtc, vtc, otc = _proj_call(xc, mods, norm_mix[l], wn, bn, wt, bt,
                                               mod_row=b, tm=tm_c, chunk=L)

        gates = jnp.concatenate([_gate_rows(gc, L), _gate_rows(gl, L)], axis=3)
        gain = jnp.broadcast_to(mlstm_norm[l][:, None], (D_MLSTM, LANES))
        ymc, yml = _mlstm_call(kc, qtc, vtc, otc, kl, qtl, vtl, otl, gates, gain,
                               ctx_out=not last)

        half = D_CONV // 2
        z_cols = _conv_call(zl, conv_w[l], ch0=0, nblk=half // LANES, seg=GRID_W)
        z_rows = _conv_call(zl, conv_w[l], ch0=half // LANES, nblk=half // LANES, stride=GRID_W)
        wm = w_out[l][:D_MLSTM].astype(BF16)
        wc = w_out[l][D_MLSTM:].astype(BF16)
        mix = (conv_b[l], conv_norm_g[l], conv_norm_b[l], wm, wc)
        x = _mixout_call(x, mods, yml, [z_cols, z_rows], *mix, mod_row=None, tm=tm_l)
        if not last:
            z_ctx = _conv_call(zc, conv_w[l], ch0=0, nblk=D_CONV // LANES, seg=zc.shape[1])
            xc = _mixout_call(xc, mods, ymc, [z_ctx], *mix, mod_row=b, tm=tm_c)

        w2 = ffn_weights(ffn2_up[l], ffn2_down[l])
        x = _ffn_call(x, mods, norm_ffn2[l], *w2, js=6, mod_row=None, tm=tm_l,
                      final_g=norm_final if last else None)
        if not last:
            xc = _ffn_call(xc, mods, norm_ffn2[l], *w2, js=6, mod_row=b, tm=tm_c)
    return x
```

```python
import functools

import jax
import jax.numpy as jnp
from jax import lax
from jax.experimental import pallas as pl
from jax.experimental.pallas import tpu as pltpu

F32 = jnp.float32
BF16 = jnp.bfloat16

GRID_W = 64
N_HEADS = 4
HEAD_DIM = 128
D_MLSTM = N_HEADS * HEAD_DIM
D_CONV = 512
CONV_WIDTH = 31
CONV_PAD = (CONV_WIDTH - 1) // 2
N_MOD = 9
EPS = 1e-6

LANES = 128
FF_CHUNK = 256
MLSTM_CHUNK = 128
FFN_UNROLL = 5
MLSTM_UNROLL = 4
STATE_ROWS = HEAD_DIM + 16
NEG_BIG = -1e30
VMEM_LIMIT = 56 * 1024 * 1024


def _cparams(n_axes):
    return pltpu.CompilerParams(
        dimension_semantics=("arbitrary",) * n_axes, vmem_limit_bytes=VMEM_LIMIT)


def _resident(shape):
    nd = len(shape)
    return pl.BlockSpec(shape, lambda *_: (0,) * nd, pipeline_mode=pl.Buffered(1))


def _rms_mod(x, g, shift, scale):
    ms = jnp.mean(x * x, axis=-1, keepdims=True)
    return (x * lax.rsqrt(ms + EPS) * g) * (1.0 + scale) + shift


def _mod_kernel(c_ref, w_ref, b_ref, o_ref):
    c = c_ref[...]
    s = (c * jax.nn.sigmoid(c)).astype(BF16)
    o_ref[0] = jnp.dot(s, w_ref[0].astype(BF16), preferred_element_type=F32) + b_ref[0]


def _mod_call(c_all, w_mod, b_mod):
    depth, d, n = w_mod.shape
    r = c_all.shape[0]
    tn = n // 8
    return pl.pallas_call(
        _mod_kernel,
        grid=(depth, n // tn),
        in_specs=[pl.BlockSpec((r, d), lambda l, j: (0, 0)),
                  pl.BlockSpec((1, d, tn), lambda l, j: (l, 0, j)),
                  pl.BlockSpec((1, 1, tn), lambda l, j: (l, 0, j))],
        out_specs=pl.BlockSpec((1, r, tn), lambda l, j: (l, 0, j)),
        out_shape=jax.ShapeDtypeStruct((depth, r, n), F32),
        compiler_params=_cparams(2),
        name="mod",
    )(c_all, w_mod, b_mod.reshape(depth, 1, n))


def _ffn_kernel(x_ref, mod_ref, ng_ref, wg_ref, wu_ref, wd_ref, *rest, js, final):
    if final:
        gf_ref, o_ref, h_ref, a_ref, acc_ref = rest
    else:
        o_ref, h_ref, a_ref, acc_ref = rest
    nck = wg_ref.shape[0]
    x = x_ref[0]
    h_ref[...] = _rms_mod(x, ng_ref[...], mod_ref[0, js], mod_ref[0, js + 1]).astype(BF16)

    def up(c):
        h = h_ref[...]
        g = jnp.dot(h, wg_ref[c], preferred_element_type=F32)
        u = jnp.dot(h, wu_ref[c], preferred_element_type=F32)
        return (g * jax.nn.sigmoid(g) * u).astype(BF16)

    a_ref[0] = up(0)
    acc_ref[...] = jnp.zeros_like(acc_ref)

    def chunk(c, carry):
        a_prev = a_ref[(c - 1) % 2]
        a_ref[c % 2] = up(c)
        acc_ref[...] += jnp.dot(a_prev, wd_ref[c - 1], preferred_element_type=F32)
        return carry

    lax.fori_loop(1, nck, chunk, 0, unroll=FFN_UNROLL)
    d_last = jnp.dot(a_ref[(nck - 1) % 2], wd_ref[nck - 1], preferred_element_type=F32)
    out = x + (0.5 * mod_ref[0, js + 2]) * (acc_ref[...] + d_last)
    if final:
        ms = jnp.mean(out * out, axis=-1, keepdims=True)
        out = out * lax.rsqrt(ms + EPS) * gf_ref[...]
    o_ref[0] = out


def _ffn_call(x, mods, ng, wg, wu, wd, *, js, mod_row, tm, final_g=None):
    b, t, d = x.shape
    nck, _, fc = wg.shape
    row = (lambda i, j: (i, 0, 0, 0)) if mod_row is None else (lambda i, j: (mod_row, 0, 0, 0))
    in_specs = [pl.BlockSpec((1, tm, d), lambda i, j: (i, j, 0)),
                pl.BlockSpec((1, N_MOD, 1, d), row),
                _resident((1, d)), _resident(wg.shape), _resident(wu.shape), _resident(wd.shape)]
    args = [x, mods, ng.reshape(1, d), wg, wu, wd]
    if final_g is not None:
        in_specs.append(_resident((1, d)))
        args.append(final_g.reshape(1, d))
    return pl.pallas_call(
        functools.partial(_ffn_kernel, js=js, final=final_g is not None),
        grid=(b, t // tm),
        in_specs=in_specs,
        out_specs=pl.BlockSpec((1, tm, d), lambda i, j: (i, j, 0)),
        out_shape=jax.ShapeDtypeStruct((b, t, d), F32),
        scratch_shapes=[pltpu.VMEM((tm, d), BF16), pltpu.VMEM((2, tm, fc), BF16),
                        pltpu.VMEM((tm, d), F32)],
        compiler_params=_cparams(2),
        name="ffn",
    )(*args)


def _proj_kernel(x_ref, mod_ref, ng_ref, wn_ref, bn_ref, wt_ref, bt_ref,
                 k_ref, z_ref, g_ref, qt_ref, vt_ref, ot_ref, *, chunk):
    tm = x_ref.shape[1]
    u = _rms_mod(x_ref[0], ng_ref[...], mod_ref[0, 3], mod_ref[0, 4]).astype(BF16)
    pn = jnp.dot(u, wn_ref[...], preferred_element_type=F32) + bn_ref[...]
    for h in range(N_HEADS):
        k_ref[0, h] = pn[:, h * HEAD_DIM:(h + 1) * HEAD_DIM].astype(BF16)
    a = pn[:, D_MLSTM:D_MLSTM + D_CONV]
    gl = pn[:, D_MLSTM + D_CONV:D_MLSTM + 2 * D_CONV]
    z_ref[0] = a * jax.nn.sigmoid(gl)
    g0 = D_MLSTM + 2 * D_CONV
    g_ref[0] = pn[:, g0:g0 + 4 * N_HEADS]
    pt = lax.dot_general(wt_ref[...], u, (((1,), (1,)), ((), ())), preferred_element_type=F32)
    scale = HEAD_DIM ** -0.5
    for h in range(N_HEADS):
        for lb in range(tm // LANES):
            c, off = divmod(lb * LANES, chunk)
            cols = slice(lb * LANES, (lb + 1) * LANES)
            dst = (0, h, c, slice(None), slice(off, off + LANES))
            rq = slice(h * HEAD_DIM, (h + 1) * HEAD_DIM)
            rv = slice(D_MLSTM + h * HEAD_DIM, D_MLSTM + (h + 1) * HEAD_DIM)
            ro = slice(2 * D_MLSTM + h * HEAD_DIM, 2 * D_MLSTM + (h + 1) * HEAD_DIM)
            qt_ref[dst] = ((pt[rq, cols] + bt_ref[rq, :]) * scale).astype(BF16)
            vt_ref[dst] = (pt[rv, cols] + bt_ref[rv, :]).astype(BF16)
            ot_ref[dst] = pt[ro, cols] + bt_ref[ro, :]


def _proj_call(x, mods, ng, wn, bn, wt, bt, *, mod_row, tm, chunk):
    b, t, d = x.shape
    nch, cpt = t // chunk, tm // chunk
    row = (lambda i, j: (i, 0, 0, 0)) if mod_row is None else (lambda i, j: (mod_row, 0, 0, 0))
    tspec = pl.BlockSpec((1, N_HEADS, cpt, HEAD_DIM, chunk), lambda i, j: (i, 0, j, 0, 0))
    tshape = (b, N_HEADS, nch, HEAD_DIM, chunk)
    return pl.pallas_call(
        functools.partial(_proj_kernel, chunk=chunk),
        grid=(b, t // tm),
        in_specs=[pl.BlockSpec((1, tm, d), lambda i, j: (i, j, 0)),
                  pl.BlockSpec((1, N_MOD, 1, d), row),
                  _resident((1, d)), _resident(wn.shape), _resident(bn.shape),
                  _resident(wt.shape), _resident(bt.shape)],
        out_specs=[pl.BlockSpec((1, N_HEADS, tm, HEAD_DIM), lambda i, j: (i, 0, j, 0)),
                   pl.BlockSpec((1, tm, D_CONV), lambda i, j: (i, j, 0)),
                   pl.BlockSpec((1, tm, 4 * N_HEADS), lambda i, j: (i, j, 0)),
                   tspec, tspec, tspec],
        out_shape=[jax.ShapeDtypeStruct((b, N_HEADS, t, HEAD_DIM), BF16),
                   jax.ShapeDtypeStruct((b, t, D_CONV), F32),
                   jax.ShapeDtypeStruct((b, t, 4 * N_HEADS), F32),
                   jax.ShapeDtypeStruct(tshape, BF16),
                   jax.ShapeDtypeStruct(tshape, BF16),
                   jax.ShapeDtypeStruct(tshape, F32)],
        compiler_params=_cparams(2),
        name="proj",
    )(x, mods, ng.reshape(1, d), wn, bn, wt, bt)


def _scan_max(r, reverse):
    n = r.shape[-1]
    lane = lax.broadcasted_iota(jnp.int32, r.shape, r.ndim - 1)
    s = 1
    while s < n:
        if reverse:
            sh = jnp.where(lane < n - s, pltpu.roll(r, n - s, r.ndim - 1), NEG_BIG)
        else:
            sh = jnp.where(lane >= s, pltpu.roll(r, s, r.ndim - 1), NEG_BIG)
        r = jnp.maximum(r, sh)
        s *= 2
    return r


def _mlstm_kernel(kc_ref, qtc_ref, vtc_ref, otc_ref, kl_ref, qtl_ref, vtl_ref, otl_ref,
                  g_ref, gain_ref, yc_ref, yl_ref, gs_ref, rt_ref, upd_ref, st_ref, *, ctx_out):
    n_c, n_l = qtc_ref.shape[2], qtl_ref.shape[2]
    n_ch = n_c + n_l
    L = qtl_ref.shape[4]
    g = g_ref[0, 0]
    srow = lax.broadcasted_iota(jnp.int32, (L, L), 0)
    tcol = lax.broadcasted_iota(jnp.int32, (L, L), 1)
    masks = (srow <= tcol, srow >= tcol)
    orders = (list(range(n_ch)),
              list(range(n_c - 1, -1, -1)) + list(range(n_ch - 1, n_c - 1, -1)))

    for d in range(2):
        li, fpre = g[2 * d], g[2 * d + 1]
        lf = jnp.minimum(fpre, 0.0) - jnp.log1p(jnp.exp(-jnp.abs(fpre)))
        tri = masks[d].astype(F32)
        cum = jnp.dot(lf, tri, preferred_element_type=F32, precision=lax.Precision.HIGHEST)
        edge = L - 1 if d == 0 else 0
        total = cum[:, edge:edge + 1]
        r = li - cum
        pm = _scan_max(r, reverse=(d == 1))
        rmax = pm[:, edge:edge + 1]
        m = jnp.zeros((1, 1), F32)
        m_prev, m_new = [None] * n_ch, [None] * n_ch
        for c in orders[d]:
            m_prev[c] = m
            m = total[c:c + 1] + jnp.maximum(m, rmax[c:c + 1])
            m_new[c] = m
        m_prev = jnp.concatenate(m_prev, axis=0)
        m_new = jnp.concatenate(m_new, axis=0)
        mx = jnp.maximum(m_prev, pm)
        alpha = -mx
        inter = jnp.exp(m_prev - mx)
        clampv = jnp.exp(-(cum + mx))
        wk = jnp.exp(total + r - m_new)
        decay = jnp.broadcast_to(jnp.exp(total + m_prev - m_new), (n_ch, L))
        rt_ref[d] = jnp.concatenate([r, jnp.zeros((LANES - n_ch, L), F32)], axis=0).T
        pad = jnp.zeros((3, L), F32)
        for c in range(n_ch):
            gs_ref[d, c] = jnp.concatenate(
                [alpha[c:c + 1], inter[c:c + 1], clampv[c:c + 1], wk[c:c + 1], decay[c:c + 1], pad],
                axis=0)

    def lat_rows(c):
        return pl.ds(pl.multiple_of(c * L, L), L)

    def increments(cidx, k, vt):
        vtf = vt.astype(F32)
        parts = []
        for d in range(2):
            wk = gs_ref[d, cidx][3:4]
            parts += [vtf * wk, wk, jnp.zeros((STATE_ROWS - HEAD_DIM - 1, L), F32)]
        lhs = jnp.concatenate(parts, axis=0).astype(BF16)
        upd = jnp.dot(lhs, k, preferred_element_type=F32)
        upd_ref[0, cidx] = upd[:STATE_ROWS]
        upd_ref[1, cidx] = upd[STATE_ROWS:]

    for c in range(n_c):
        increments(c, kc_ref[0, 0, c * L:(c + 1) * L, :], vtc_ref[0, 0, c])

    def inc_latent(c, carry):
        increments(n_c + c, kl_ref[0, 0, lat_rows(c), :], vtl_ref[0, 0, c])
        return carry

    lax.fori_loop(0, n_l, inc_latent, 0, unroll=MLSTM_UNROLL)

    for d in range(2):
        state = jnp.zeros((STATE_ROWS, HEAD_DIM), F32)
        for c in orders[d]:
            st_ref[d, c] = state.astype(BF16)
            state = gs_ref[d, c][4:5, :HEAD_DIM] * state + upd_ref[d, c]

    gain = gain_ref[...]

    def outputs(cidx, k, qt, vt, ot):
        s_t = jnp.dot(k, qt, preferred_element_type=F32)
        carried = jnp.dot(jnp.concatenate([st_ref[0, cidx], st_ref[1, cidx]], axis=0), qt,
                          preferred_element_type=F32)
        sws, dens, rows = [], [], []
        for d in range(2):
            gsl = gs_ref[d, cidx]
            rows.append(gsl)
            arg = jnp.where(masks[d], rt_ref[d, :, cidx:cidx + 1] + gsl[0:1], NEG_BIG)
            sw = s_t * jnp.exp(arg)
            sws.append(sw.astype(BF16))
            qn = carried[d * STATE_ROWS + HEAD_DIM:d * STATE_ROWS + HEAD_DIM + 1]
            dens.append(jnp.sum(sw, axis=0, keepdims=True) + gsl[1:2] * qn)
        num2 = jnp.dot(vt, jnp.concatenate(sws, axis=1), preferred_element_type=F32)
        hsum = None
        for d in range(2):
            num = (num2[:, d * L:(d + 1) * L]
                   + rows[d][1:2] * carried[d * STATE_ROWS:d * STATE_ROWS + HEAD_DIM])
            h = num / jnp.maximum(jnp.abs(dens[d]), rows[d][2:3])
            hsum = h if hsum is None else hsum + h
        ms = jnp.mean(hsum * hsum, axis=0, keepdims=True)
        y = hsum * lax.rsqrt(ms + EPS) * gain * jax.nn.sigmoid(ot)
        return y.T.astype(BF16)

    if ctx_out:
        for c in range(n_c):
            yc_ref[0, c * L:(c + 1) * L, :] = outputs(
                c, kc_ref[0, 0, c * L:(c + 1) * L, :], qtc_ref[0, 0, c], vtc_ref[0, 0, c],
                otc_ref[0, 0, c])
    else:
        yc_ref[...] = jnp.zeros_like(yc_ref)

    for c in range(n_l):
        yl_ref[0, c * L:(c + 1) * L, :] = outputs(
            n_c + c, kl_ref[0, 0, c * L:(c + 1) * L, :], qtl_ref[0, 0, c], vtl_ref[0, 0, c],
            otl_ref[0, 0, c])


def _mlstm_call(kc, qtc, vtc, otc, kl, qtl, vtl, otl, gates, gain, *, ctx_out):
    b = kl.shape[0]
    tc, tl = kc.shape[2], kl.shape[2]
    n_c, n_l, L = qtc.shape[2], qtl.shape[2], qtl.shape[4]
    n_ch = n_c + n_l
    assert L == LANES and n_ch <= L

    def nat(t):
        return pl.BlockSpec((1, 1, t, HEAD_DIM), lambda i, h: (i, h, 0, 0))

    def tr(n):
        return pl.BlockSpec((1, 1, n, HEAD_DIM, L), lambda i, h: (i, h, 0, 0, 0))

    return pl.pallas_call(
        functools.partial(_mlstm_kernel, ctx_out=ctx_out),
        grid=(b, N_HEADS),
        in_specs=[nat(tc), tr(n_c), tr(n_c), tr(n_c), nat(tl), tr(n_l), tr(n_l), tr(n_l),
                  pl.BlockSpec((1, 1, 4, n_ch, L), lambda i, h: (i, h, 0, 0, 0)),
                  pl.BlockSpec((HEAD_DIM, LANES), lambda i, h: (h, 0))],
        out_specs=[pl.BlockSpec((1, tc, HEAD_DIM), lambda i, h: (i, 0, h)),
                   pl.BlockSpec((1, tl, HEAD_DIM), lambda i, h: (i, 0, h))],
        out_shape=[jax.ShapeDtypeStruct((b, tc, D_MLSTM), BF16),
                   jax.ShapeDtypeStruct((b, tl, D_MLSTM), BF16)],
        scratch_shapes=[pltpu.VMEM((2, n_ch, 8, L), F32),
                        pltpu.VMEM((2, L, LANES), F32),
                        pltpu.VMEM((2, n_ch, STATE_ROWS, HEAD_DIM), F32),
                        pltpu.VMEM((2, n_ch, STATE_ROWS, HEAD_DIM), BF16)],
        compiler_params=_cparams(2),
        name="mlstm",
    )(kc, qtc, vtc, otc, kl, qtl, vtl, otl, gates, gain)


SEG_LPAD = 16
ROW_TILE = 64


def _conv_seg_kernel(z_ref, w_ref, o_ref, zp_ref, *, seg):
    t = z_ref.shape[1]
    pitch = seg + 2 * SEG_LPAD
    zp_ref[...] = jnp.zeros_like(zp_ref)
    for s in range(t // seg):
        zp_ref[s * pitch + SEG_LPAD:s * pitch + SEG_LPAD + seg, :] = z_ref[0, s * seg:(s + 1) * seg, :]
    tiles = seg // ROW_TILE

    def tile(i, carry):
        s, q = i // tiles, i % tiles
        src = s * pitch + q * ROW_TILE + (SEG_LPAD - CONV_PAD)
        acc = jnp.zeros((ROW_TILE, LANES), F32)
        for j in range(CONV_WIDTH):
            acc = acc + w_ref[j:j + 1, :] * zp_ref[pl.ds(src + j, ROW_TILE), :]
        o_ref[0, pl.ds(pl.multiple_of(i * ROW_TILE, ROW_TILE), ROW_TILE), :] = acc
        return carry

    lax.fori_loop(0, t // ROW_TILE, tile, 0)


def _conv_stride_kernel(z_ref, w_ref, o_ref, zp_ref, *, stride):
    t = z_ref.shape[1]
    pad = CONV_PAD * stride
    zp_ref[0:pad, :] = jnp.zeros((pad, LANES), F32)
    zp_ref[pad + t:, :] = jnp.zeros((pad, LANES), F32)
    zp_ref[pad:pad + t, :] = z_ref[0]

    def tile(i, carry):
        base = pl.multiple_of(i * ROW_TILE, ROW_TILE)
        acc = jnp.zeros((ROW_TILE, LANES), F32)
        for j in range(CONV_WIDTH):
            acc = acc + w_ref[j:j + 1, :] * zp_ref[pl.ds(base + j * stride, ROW_TILE), :]
        o_ref[0, pl.ds(base, ROW_TILE), :] = acc
        return carry

    lax.fori_loop(0, t // ROW_TILE, tile, 0)


def _conv_call(z, w, *, ch0, nblk, seg=None, stride=None):
    b, t, _ = z.shape
    if seg is not None:
        body = functools.partial(_conv_seg_kernel, seg=seg)
        rows = (t // seg) * (seg + 2 * SEG_LPAD)
    else:
        body = functools.partial(_conv_stride_kernel, stride=stride)
        rows = t + 2 * CONV_PAD * stride
    return pl.pallas_call(
        body,
        grid=(b, nblk),
        in_specs=[pl.BlockSpec((1, t, LANES), lambda i, j: (i, 0, ch0 + j)),
                  pl.BlockSpec((CONV_WIDTH, LANES), lambda i, j: (0, ch0 + j))],
        out_specs=pl.BlockSpec((1, t, LANES), lambda i, j: (i, 0, j)),
        out_shape=jax.ShapeDtypeStruct((b, t, nblk * LANES), F32),
        scratch_shapes=[pltpu.VMEM((rows, LANES), F32)],
        compiler_params=_cparams(2),
        name="conv",
    )(z, w)


def _mixout_kernel(*refs, nz):
    x_ref, mod_ref, ym_ref = refs[:3]
    z_refs = refs[3:3 + nz]
    cb_ref, cg_ref, cnb_ref, wm_ref, wc_ref, o_ref = refs[3 + nz:]
    z = jnp.concatenate([r[0] for r in z_refs], axis=-1) if nz > 1 else z_refs[0][0]
    z = z + cb_ref[...]
    mu = jnp.mean(z, axis=-1, keepdims=True)
    zc = z - mu
    var = jnp.mean(zc * zc, axis=-1, keepdims=True)
    ln = zc * lax.rsqrt(var + EPS) * cg_ref[...] + cnb_ref[...]
    yc = (ln * jax.nn.sigmoid(ln)).astype(BF16)
    y = (jnp.dot(ym_ref[0], wm_ref[...], preferred_element_type=F32)
         + jnp.dot(yc, wc_ref[...], preferred_element_type=F32))
    o_ref[0] = x_ref[0] + mod_ref[0, 5] * y


def _mixout_call(x, mods, ym, zs, cb, cg, cnb, wm, wc, *, mod_row, tm):
    b, t, d = x.shape
    row = (lambda i, j: (i, 0, 0, 0)) if mod_row is None else (lambda i, j: (mod_row, 0, 0, 0))

    def tok(c):
        return pl.BlockSpec((1, tm, c), lambda i, j: (i, j, 0))

    return pl.pallas_call(
        functools.partial(_mixout_kernel, nz=len(zs)),
        grid=(b, t // tm),
        in_specs=[tok(d), pl.BlockSpec((1, N_MOD, 1, d), row), tok(D_MLSTM)]
                 + [tok(z.shape[2]) for z in zs]
                 + [_resident((1, D_CONV))] * 3 + [_resident(wm.shape), _resident(wc.shape)],
        out_specs=tok(d),
        out_shape=jax.ShapeDtypeStruct((b, t, d), F32),
        compiler_params=_cparams(2),
        name="mixout",
    )(x, mods, ym, *zs, cb.reshape(1, -1), cg.reshape(1, -1), cnb.reshape(1, -1), wm, wc)


def _chunked_cols(w, n):
    k = w.shape[0]
    return w.reshape(k, n, FF_CHUNK).transpose(1, 0, 2)


def _gate_rows(g, chunk):
    b, t, _ = g.shape
    return g.reshape(b, t // chunk, chunk, 4, N_HEADS).transpose(0, 4, 3, 1, 2)


def kernel(x, c, ctx, c_ctx, w_mod, b_mod, norm_ffn1, ffn1_up, ffn1_down, norm_mix, w_in, b_in,
           mlstm_norm, conv_w, conv_b, conv_norm_g, conv_norm_b, w_out, norm_ffn2, ffn2_up,
           ffn2_down, norm_final):
    b, s, d = x.shape
    depth = w_mod.shape[0]
    d_ff = ffn1_down.shape[1]
    nck = d_ff // FF_CHUNK
    tm_l, tm_c = 512, ctx.shape[1]
    L = MLSTM_CHUNK

    rows = -(-(b + 1) // 8) * 8
    c_all = jnp.concatenate([c, c_ctx[None, :], jnp.zeros((rows - b - 1, d), F32)], axis=0)
    mods_all = _mod_call(c_all, w_mod, b_mod).reshape(depth, rows, N_MOD, 1, d)

    q0, k0, v0, o0 = 0, D_MLSTM, 2 * D_MLSTM, 3 * D_MLSTM
    g0 = 4 * D_MLSTM
    u0 = g0 + 4 * N_HEADS
    n_nat = D_MLSTM + 2 * D_CONV + LANES

    xc = ctx
    for l in range(depth):
        last = l == depth - 1
        mods = mods_all[l]

        def ffn_weights(up, down):
            return (_chunked_cols(up[:, :d_ff], nck).astype(BF16),
                    _chunked_cols(up[:, d_ff:], nck).astype(BF16),
                    down.reshape(nck, FF_CHUNK, d).astype(BF16))

        w1 = ffn_weights(ffn1_up[l], ffn1_down[l])
        x = _ffn_call(x, mods, norm_ffn1[l], *w1, js=0, mod_row=None, tm=tm_l)
        xc = _ffn_call(xc, mods, norm_ffn1[l], *w1, js=0, mod_row=b, tm=tm_c)

        wi, bi = w_in[l], b_in[l]
        wn = jnp.concatenate([wi[:, k0:v0], wi[:, u0:], wi[:, g0:u0],
                              jnp.zeros((d, LANES - 4 * N_HEADS), F32)], axis=1).astype(BF16)
        bn = jnp.concatenate([bi[k0:v0], bi[u0:], bi[g0:u0],
                              jnp.zeros((LANES - 4 * N_HEADS,), F32)]).reshape(1, n_nat)
        wt = jnp.concatenate([wi[:, q0:k0], wi[:, v0:o0], wi[:, o0:g0]], axis=1).T.astype(BF16)
        bt = jnp.broadcast_to(
            jnp.concatenate([bi[q0:k0], bi[v0:o0], bi[o0:g0]])[:, None], (3 * D_MLSTM, LANES))
        kl, zl, gl, qtl, vtl, otl = _proj_call(x, mods, norm_mix[l], wn, bn, wt, bt,
                                               mod_row=None, tm=tm_l, chunk=L)
        kc, zc, gc, qtc, vtc, otc = _proj_call(xc, mods, norm_mix[l], wn, bn, wt, bt,
                                               mod_row=b, tm=tm_c, chunk=L)

        gates = jnp.concatenate([_gate_rows(gc, L), _gate_rows(gl, L)], axis=3)
        gain = jnp.broadcast_to(mlstm_norm[l][:, None], (D_MLSTM, LANES))
        ymc, yml = _mlstm_call(kc, qtc, vtc, otc, kl, qtl, vtl, otl, gates, gain,
                               ctx_out=not last)

        half = D_CONV // 2
        z_cols = _conv_call(zl, conv_w[l], ch0=0, nblk=half // LANES, seg=GRID_W)
        z_rows = _conv_call(zl, conv_w[l], ch0=half // LANES, nblk=half // LANES, stride=GRID_W)
        wm = w_out[l][:D_MLSTM].astype(BF16)
        wc = w_out[l][D_MLSTM:].astype(BF16)
        mix = (conv_b[l], conv_norm_g[l], conv_norm_b[l], wm, wc)
        x = _mixout_call(x, mods, yml, [z_cols, z_rows], *mix, mod_row=None, tm=tm_l)
        if not last:
            z_ctx = _conv_call(zc, conv_w[l], ch0=0, nblk=D_CONV // LANES, seg=zc.shape[1])
            xc = _mixout_call(xc, mods, ymc, [z_ctx], *mix, mod_row=b, tm=tm_c)

        w2 = ffn_weights(ffn2_up[l], ffn2_down[l])
        x = _ffn_call(x, mods, norm_ffn2[l], *w2, js=6, mod_row=None, tm=tm_l,
                      final_g=norm_final if last else None)
        if not last:
            xc = _ffn_call(xc, mods, norm_ffn2[l], *w2, js=6, mod_row=b, tm=tm_c)
    return x
```

```python
import functools

import jax
import jax.numpy as jnp
from jax import lax
from jax.experimental import pallas as pl
from jax.experimental.pallas import tpu as pltpu

F32 = jnp.float32
BF16 = jnp.bfloat16

GRID_W = 64
N_HEADS = 4
HEAD_DIM = 128
D_MLSTM = N_HEADS * HEAD_DIM
D_CONV = 512
CONV_WIDTH = 31
CONV_PAD = (CONV_WIDTH - 1) // 2
N_MOD = 9
EPS = 1e-6

LANES = 128
FF_CHUNK = 256
FFN_TILE = 1024
PROJ_TILE = 512
MLSTM_CHUNK = 128
FFN_UNROLL = 5
STATE_ROWS = HEAD_DIM + 16
NEG_BIG = -1e30
VMEM_LIMIT = 56 * 1024 * 1024


def _cparams(n_axes):
    return pltpu.CompilerParams(
        dimension_semantics=("arbitrary",) * n_axes, vmem_limit_bytes=VMEM_LIMIT)


def _resident(shape):
    nd = len(shape)
    return pl.BlockSpec(shape, lambda *_: (0,) * nd, pipeline_mode=pl.Buffered(1))


def _rms_mod(x, g, shift, scale):
    ms = jnp.mean(x * x, axis=-1, keepdims=True)
    return (x * lax.rsqrt(ms + EPS) * g) * (1.0 + scale) + shift


def _mod_kernel(c_ref, w_ref, b_ref, o_ref):
    c = c_ref[...]
    s = (c * jax.nn.sigmoid(c)).astype(BF16)
    o_ref[0] = jnp.dot(s, w_ref[0].astype(BF16), preferred_element_type=F32) + b_ref[0]


def _mod_call(c_all, w_mod, b_mod):
    depth, d, n = w_mod.shape
    r = c_all.shape[0]
    tn = n // 8
    return pl.pallas_call(
        _mod_kernel,
        grid=(depth, n // tn),
        in_specs=[pl.BlockSpec((r, d), lambda l, j: (0, 0)),
                  pl.BlockSpec((1, d, tn), lambda l, j: (l, 0, j)),
                  pl.BlockSpec((1, 1, tn), lambda l, j: (l, 0, j))],
        out_specs=pl.BlockSpec((1, r, tn), lambda l, j: (l, 0, j)),
        out_shape=jax.ShapeDtypeStruct((depth, r, n), F32),
        compiler_params=_cparams(2),
        name="mod",
    )(c_all, w_mod, b_mod.reshape(depth, 1, n))


def _ffn_kernel(*refs, js, final, nz):
    x_ref, mod_ref = refs[:2]
    refs = refs[2:]
    if nz:
        ym_ref, z_refs = refs[0], refs[1:1 + nz]
        cb_ref, cg_ref, cnb_ref, wm_ref, wc_ref = refs[1 + nz:6 + nz]
        refs = refs[6 + nz:]
    ng_ref, wg_ref, wu_ref, wd_ref = refs[:4]
    refs = refs[4:]
    if final:
        gf_ref, refs = refs[0], refs[1:]
    o_ref, h_ref, a_ref, acc_ref = refs
    nck = wg_ref.shape[0]

    x = x_ref[0]
    if nz:
        z = jnp.concatenate([r[0] for r in z_refs], axis=-1) if nz > 1 else z_refs[0][0]
        z = z + cb_ref[...]
        zc = z - jnp.mean(z, axis=-1, keepdims=True)
        var = jnp.mean(zc * zc, axis=-1, keepdims=True)
        ln = zc * lax.rsqrt(var + EPS) * cg_ref[...] + cnb_ref[...]
        yc = (ln * jax.nn.sigmoid(ln)).astype(BF16)
        y = (jnp.dot(ym_ref[0], wm_ref[...], preferred_element_type=F32)
             + jnp.dot(yc, wc_ref[...], preferred_element_type=F32))
        x = x + mod_ref[0, 5] * y
    o_ref[0] = x
    h_ref[...] = _rms_mod(x, ng_ref[...], mod_ref[0, js], mod_ref[0, js + 1]).astype(BF16)

    def up(c):
        h = h_ref[...]
        g = jnp.dot(h, wg_ref[c], preferred_element_type=F32)
        u = jnp.dot(h, wu_ref[c], preferred_element_type=F32)
        return (g * jax.nn.sigmoid(g) * u).astype(BF16)

    a_ref[0] = up(0)
    acc_ref[...] = jnp.zeros_like(acc_ref)

    def chunk(c, carry):
        a_prev = a_ref[(c - 1) % 2]
        a_ref[c % 2] = up(c)
        acc_ref[...] += jnp.dot(a_prev, wd_ref[c - 1], preferred_element_type=F32)
        return carry

    lax.fori_loop(1, nck, chunk, 0, unroll=FFN_UNROLL)
    d_last = jnp.dot(a_ref[(nck - 1) % 2], wd_ref[nck - 1], preferred_element_type=F32)
    out = o_ref[0] + (0.5 * mod_ref[0, js + 2]) * (acc_ref[...] + d_last)
    if final:
        ms = jnp.mean(out * out, axis=-1, keepdims=True)
        out = out * lax.rsqrt(ms + EPS) * gf_ref[...]
    o_ref[0] = out


def _ffn_call(x, mods, ng, wg, wu, wd, *, js, mod_row, tm, final_g=None, mix=None):
    b, t, d = x.shape
    nck, _, fc = wg.shape
    row = (lambda i, j: (i, 0, 0, 0)) if mod_row is None else (lambda i, j: (mod_row, 0, 0, 0))

    def tok(c):
        return pl.BlockSpec((1, tm, c), lambda i, j: (i, j, 0))

    in_specs = [tok(d), pl.BlockSpec((1, N_MOD, 1, d), row)]
    args = [x, mods]
    nz = 0
    if mix is not None:
        ym, zs, cb, cg, cnb, wm, wc = mix
        nz = len(zs)
        in_specs += [tok(ym.shape[2])] + [tok(z.shape[2]) for z in zs]
        in_specs += [_resident((1, D_CONV))] * 3 + [_resident(wm.shape), _resident(wc.shape)]
        args += [ym, *zs, cb.reshape(1, -1), cg.reshape(1, -1), cnb.reshape(1, -1), wm, wc]
    in_specs += [_resident((1, d)), _resident(wg.shape), _resident(wu.shape), _resident(wd.shape)]
    args += [ng.reshape(1, d), wg, wu, wd]
    if final_g is not None:
        in_specs.append(_resident((1, d)))
        args.append(final_g.reshape(1, d))
    return pl.pallas_call(
        functools.partial(_ffn_kernel, js=js, final=final_g is not None, nz=nz),
        grid=(b, t // tm),
        in_specs=in_specs,
        out_specs=tok(d),
        out_shape=jax.ShapeDtypeStruct((b, t, d), F32),
        scratch_shapes=[pltpu.VMEM((tm, d), BF16), pltpu.VMEM((2, tm, fc), BF16),
                        pltpu.VMEM((tm, d), F32)],
        compiler_params=_cparams(2),
        name="ffn",
    )(*args)


def _proj_kernel(x_ref, mod_ref, ng_ref, wn_ref, bn_ref, wt_ref, bt_ref,
                 k_ref, z_ref, g_ref, qt_ref, vt_ref, ot_ref, *, chunk):
    tm = x_ref.shape[1]
    u = _rms_mod(x_ref[0], ng_ref[...], mod_ref[0, 3], mod_ref[0, 4]).astype(BF16)
    pn = jnp.dot(u, wn_ref[...], preferred_element_type=F32) + bn_ref[...]
    for h in range(N_HEADS):
        k_ref[0, h] = pn[:, h * HEAD_DIM:(h + 1) * HEAD_DIM].astype(BF16)
    a = pn[:, D_MLSTM:D_MLSTM + D_CONV]
    gl = pn[:, D_MLSTM + D_CONV:D_MLSTM + 2 * D_CONV]
    z_ref[0] = a * jax.nn.sigmoid(gl)
    g0 = D_MLSTM + 2 * D_CONV
    g_ref[0] = pn[:, g0:g0 + 4 * N_HEADS]
    pt = lax.dot_general(wt_ref[...], u, (((1,), (1,)), ((), ())), preferred_element_type=F32)
    scale = HEAD_DIM ** -0.5
    for h in range(N_HEADS):
        for lb in range(tm // LANES):
            c, off = divmod(lb * LANES, chunk)
            cols = slice(lb * LANES, (lb + 1) * LANES)
            dst = (0, h, c, slice(None), slice(off, off + LANES))
            rq = slice(h * HEAD_DIM, (h + 1) * HEAD_DIM)
            rv = slice(D_MLSTM + h * HEAD_DIM, D_MLSTM + (h + 1) * HEAD_DIM)
            ro = slice(2 * D_MLSTM + h * HEAD_DIM, 2 * D_MLSTM + (h + 1) * HEAD_DIM)
            qt_ref[dst] = ((pt[rq, cols] + bt_ref[rq, :]) * scale).astype(BF16)
            vt_ref[dst] = (pt[rv, cols] + bt_ref[rv, :]).astype(BF16)
            ot_ref[dst] = pt[ro, cols] + bt_ref[ro, :]


def _proj_call(x, mods, ng, wn, bn, wt, bt, *, mod_row, tm, chunk):
    b, t, d = x.shape
    nch, cpt = t // chunk, tm // chunk
    row = (lambda i, j: (i, 0, 0, 0)) if mod_row is None else (lambda i, j: (mod_row, 0, 0, 0))
    tspec = pl.BlockSpec((1, N_HEADS, cpt, HEAD_DIM, chunk), lambda i, j: (i, 0, j, 0, 0))
    tshape = (b, N_HEADS, nch, HEAD_DIM, chunk)
    return pl.pallas_call(
        functools.partial(_proj_kernel, chunk=chunk),
        grid=(b, t // tm),
        in_specs=[pl.BlockSpec((1, tm, d), lambda i, j: (i, j, 0)),
                  pl.BlockSpec((1, N_MOD, 1, d), row),
                  _resident((1, d)), _resident(wn.shape), _resident(bn.shape),
                  _resident(wt.shape), _resident(bt.shape)],
        out_specs=[pl.BlockSpec((1, N_HEADS, tm, HEAD_DIM), lambda i, j: (i, 0, j, 0)),
                   pl.BlockSpec((1, tm, D_CONV), lambda i, j: (i, j, 0)),
                   pl.BlockSpec((1, tm, 4 * N_HEADS), lambda i, j: (i, j, 0)),
                   tspec, tspec, tspec],
        out_shape=[jax.ShapeDtypeStruct((b, N_HEADS, t, HEAD_DIM), BF16),
                   jax.ShapeDtypeStruct((b, t, D_CONV), F32),
                   jax.ShapeDtypeStruct((b, t, 4 * N_HEADS), F32),
                   jax.ShapeDtypeStruct(tshape, BF16),
                   jax.ShapeDtypeStruct(tshape, BF16),
                   jax.ShapeDtypeStruct(tshape, F32)],
        compiler_params=_cparams(2),
        name="proj",
    )(x, mods, ng.reshape(1, d), wn, bn, wt, bt)


def _scan_max(r, reverse):
    n = r.shape[-1]
    lane = lax.broadcasted_iota(jnp.int32, r.shape, r.ndim - 1)
    s = 1
    while s < n:
        if reverse:
            sh = jnp.where(lane < n - s, pltpu.roll(r, n - s, r.ndim - 1), NEG_BIG)
        else:
            sh = jnp.where(lane >= s, pltpu.roll(r, s, r.ndim - 1), NEG_BIG)
        r = jnp.maximum(r, sh)
        s *= 2
    return r


def _mlstm_kernel(kc_ref, qtc_ref, vtc_ref, otc_ref, kl_ref, qtl_ref, vtl_ref, otl_ref,
                  g_ref, gain_ref, yc_ref, yl_ref, gs_ref, rt_ref, upd_ref, st_ref, *, ctx_out):
    n_c, n_l = qtc_ref.shape[2], qtl_ref.shape[2]
    n_ch = n_c + n_l
    L = qtl_ref.shape[4]
    g = g_ref[0, 0]
    srow = lax.broadcasted_iota(jnp.int32, (L, L), 0)
    tcol = lax.broadcasted_iota(jnp.int32, (L, L), 1)
    masks = (srow <= tcol, srow >= tcol)
    orders = (list(range(n_ch)),
              list(range(n_c - 1, -1, -1)) + list(range(n_ch - 1, n_c - 1, -1)))

    for d in range(2):
        li, fpre = g[2 * d], g[2 * d + 1]
        lf = jnp.minimum(fpre, 0.0) - jnp.log1p(jnp.exp(-jnp.abs(fpre)))
        tri = masks[d].astype(F32)
        cum = jnp.dot(lf, tri, preferred_element_type=F32, precision=lax.Precision.HIGHEST)
        edge = L - 1 if d == 0 else 0
        total = cum[:, edge:edge + 1]
        r = li - cum
        pm = _scan_max(r, reverse=(d == 1))
        rmax = pm[:, edge:edge + 1]
        m = jnp.zeros((1, 1), F32)
        m_prev, m_new = [None] * n_ch, [None] * n_ch
        for c in orders[d]:
            m_prev[c] = m
            m = total[c:c + 1] + jnp.maximum(m, rmax[c:c + 1])
            m_new[c] = m
        m_prev = jnp.concatenate(m_prev, axis=0)
        m_new = jnp.concatenate(m_new, axis=0)
        mx = jnp.maximum(m_prev, pm)
        alpha = -mx
        inter = jnp.exp(m_prev - mx)
        clampv = jnp.exp(-(cum + mx))
        wk = jnp.exp(total + r - m_new)
        decay = jnp.broadcast_to(jnp.exp(total + m_prev - m_new), (n_ch, L))
        rt_ref[d] = jnp.concatenate([r, jnp.zeros((LANES - n_ch, L), F32)], axis=0).T
        pad = jnp.zeros((3, L), F32)
        for c in range(n_ch):
            gs_ref[d, c] = jnp.concatenate(
                [alpha[c:c + 1], inter[c:c + 1], clampv[c:c + 1], wk[c:c + 1], decay[c:c + 1], pad],
                axis=0)

    def increments(cidx, k, vt):
        vtf = vt.astype(F32)
        parts = []
        for d in range(2):
            wk = gs_ref[d, cidx][3:4]
            parts += [vtf * wk, wk, jnp.zeros((STATE_ROWS - HEAD_DIM - 1, L), F32)]
        lhs = jnp.concatenate(parts, axis=0).astype(BF16)
        upd = jnp.dot(lhs, k, preferred_element_type=F32)
        upd_ref[0, cidx] = upd[:STATE_ROWS]
        upd_ref[1, cidx] = upd[STATE_ROWS:]

    for c in range(n_c):
        increments(c, kc_ref[0, 0, c * L:(c + 1) * L, :], vtc_ref[0, 0, c])
    for c in range(n_l):
        increments(n_c + c, kl_ref[0, 0, c * L:(c + 1) * L, :], vtl_ref[0, 0, c])

    for d in range(2):
        state = jnp.zeros((STATE_ROWS, HEAD_DIM), F32)
        for c in orders[d]:
            st_ref[d, c] = state.astype(BF16)
            state = gs_ref[d, c][4:5, :HEAD_DIM] * state + upd_ref[d, c]

    gain = gain_ref[...]

    def outputs(cidx, k, qt, vt, ot):
        s_t = jnp.dot(k, qt, preferred_element_type=F32)
        carried = jnp.dot(jnp.concatenate([st_ref[0, cidx], st_ref[1, cidx]], axis=0), qt,
                          preferred_element_type=F32)
        sws, dens, rows = [], [], []
        for d in range(2):
            gsl = gs_ref[d, cidx]
            rows.append(gsl)
            arg = jnp.where(masks[d], rt_ref[d, :, cidx:cidx + 1] + gsl[0:1], NEG_BIG)
            sw = s_t * jnp.exp(arg)
            sws.append(sw.astype(BF16))
            qn = carried[d * STATE_ROWS + HEAD_DIM:d * STATE_ROWS + HEAD_DIM + 1]
            dens.append(jnp.sum(sw, axis=0, keepdims=True) + gsl[1:2] * qn)
        num2 = jnp.dot(vt, jnp.concatenate(sws, axis=1), preferred_element_type=F32)
        hsum = None
        for d in range(2):
            num = (num2[:, d * L:(d + 1) * L]
                   + rows[d][1:2] * carried[d * STATE_ROWS:d * STATE_ROWS + HEAD_DIM])
            h = num * (1.0 / jnp.maximum(jnp.abs(dens[d]), rows[d][2:3]))
            hsum = h if hsum is None else hsum + h
        ms = jnp.mean(hsum * hsum, axis=0, keepdims=True)
        y = hsum * lax.rsqrt(ms + EPS) * gain * jax.nn.sigmoid(ot)
        return y.T.astype(BF16)

    if ctx_out:
        for c in range(n_c):
            yc_ref[0, c * L:(c + 1) * L, :] = outputs(
                c, kc_ref[0, 0, c * L:(c + 1) * L, :], qtc_ref[0, 0, c], vtc_ref[0, 0, c],
                otc_ref[0, 0, c])
    else:
        yc_ref[...] = jnp.zeros_like(yc_ref)

    for c in range(n_l):
        yl_ref[0, c * L:(c + 1) * L, :] = outputs(
            n_c + c, kl_ref[0, 0, c * L:(c + 1) * L, :], qtl_ref[0, 0, c], vtl_ref[0, 0, c],
            otl_ref[0, 0, c])


def _mlstm_call(kc, qtc, vtc, otc, kl, qtl, vtl, otl, gates, gain, *, ctx_out):
    b = kl.shape[0]
    tc, tl = kc.shape[2], kl.shape[2]
    n_c, n_l, L = qtc.shape[2], qtl.shape[2], qtl.shape[4]
    n_ch = n_c + n_l
    assert L == LANES and n_ch <= L

    def nat(t):
        return pl.BlockSpec((1, 1, t, HEAD_DIM), lambda i, h: (i, h, 0, 0))

    def tr(n):
        return pl.BlockSpec((1, 1, n, HEAD_DIM, L), lambda i, h: (i, h, 0, 0, 0))

    return pl.pallas_call(
        functools.partial(_mlstm_kernel, ctx_out=ctx_out),
        grid=(b, N_HEADS),
        in_specs=[nat(tc), tr(n_c), tr(n_c), tr(n_c), nat(tl), tr(n_l), tr(n_l), tr(n_l),
                  pl.BlockSpec((1, 1, 4, n_ch, L), lambda i, h: (i, h, 0, 0, 0)),
                  pl.BlockSpec((HEAD_DIM, LANES), lambda i, h: (h, 0))],
        out_specs=[pl.BlockSpec((1, tc, HEAD_DIM), lambda i, h: (i, 0, h)),
                   pl.BlockSpec((1, tl, HEAD_DIM), lambda i, h: (i, 0, h))],
        out_shape=[jax.ShapeDtypeStruct((b, tc, D_MLSTM), BF16),
                   jax.ShapeDtypeStruct((b, tl, D_MLSTM), BF16)],
        scratch_shapes=[pltpu.VMEM((2, n_ch, 8, L), F32),
                        pltpu.VMEM((2, L, LANES), F32),
                        pltpu.VMEM((2, n_ch, STATE_ROWS, HEAD_DIM), F32),
                        pltpu.VMEM((2, n_ch, STATE_ROWS, HEAD_DIM), BF16)],
        compiler_params=_cparams(2),
        name="mlstm",
    )(kc, qtc, vtc, otc, kl, qtl, vtl, otl, gates, gain)


SEG_LPAD = 16
ROW_TILE = 64


def _conv_seg_kernel(z_ref, w_ref, o_ref, zp_ref, *, seg):
    t = z_ref.shape[1]
    pitch = seg + 2 * SEG_LPAD
    zp_ref[...] = jnp.zeros_like(zp_ref)
    for s in range(t // seg):
        zp_ref[s * pitch + SEG_LPAD:s * pitch + SEG_LPAD + seg, :] = z_ref[0, s * seg:(s + 1) * seg, :]
    tiles = seg // ROW_TILE

    def tile(i, carry):
        s, q = i // tiles, i % tiles
        src = s * pitch + q * ROW_TILE + (SEG_LPAD - CONV_PAD)
        acc = jnp.zeros((ROW_TILE, LANES), F32)
        for j in range(CONV_WIDTH):
            acc = acc + w_ref[j:j + 1, :] * zp_ref[pl.ds(src + j, ROW_TILE), :]
        o_ref[0, pl.ds(pl.multiple_of(i * ROW_TILE, ROW_TILE), ROW_TILE), :] = acc
        return carry

    lax.fori_loop(0, t // ROW_TILE, tile, 0)


def _conv_stride_kernel(z_ref, w_ref, o_ref, zp_ref, *, stride):
    t = z_ref.shape[1]
    pad = CONV_PAD * stride
    zp_ref[0:pad, :] = jnp.zeros((pad, LANES), F32)
    zp_ref[pad + t:, :] = jnp.zeros((pad, LANES), F32)
    zp_ref[pad:pad + t, :] = z_ref[0]

    def tile(i, carry):
        base = pl.multiple_of(i * ROW_TILE, ROW_TILE)
        acc = jnp.zeros((ROW_TILE, LANES), F32)
        for j in range(CONV_WIDTH):
            acc = acc + w_ref[j:j + 1, :] * zp_ref[pl.ds(base + j * stride, ROW_TILE), :]
        o_ref[0, pl.ds(base, ROW_TILE), :] = acc
        return carry

    lax.fori_loop(0, t // ROW_TILE, tile, 0)


def _conv_call(z, w, *, ch0, nblk, seg=None, stride=None):
    b, t, _ = z.shape
    if seg is not None:
        body = functools.partial(_conv_seg_kernel, seg=seg)
        rows = (t // seg) * (seg + 2 * SEG_LPAD)
    else:
        body = functools.partial(_conv_stride_kernel, stride=stride)
        rows = t + 2 * CONV_PAD * stride
    return pl.pallas_call(
        body,
        grid=(b, nblk),
        in_specs=[pl.BlockSpec((1, t, LANES), lambda i, j: (i, 0, ch0 + j)),
                  pl.BlockSpec((CONV_WIDTH, LANES), lambda i, j: (0, ch0 + j))],
        out_specs=pl.BlockSpec((1, t, LANES), lambda i, j: (i, 0, j)),
        out_shape=jax.ShapeDtypeStruct((b, t, nblk * LANES), F32),
        scratch_shapes=[pltpu.VMEM((rows, LANES), F32)],
        compiler_params=_cparams(2),
        name="conv",
    )(z, w)


def _chunked_cols(w, n):
    k = w.shape[0]
    return w.reshape(k, n, FF_CHUNK).transpose(1, 0, 2)


def _gate_rows(g, chunk):
    b, t, _ = g.shape
    return g.reshape(b, t // chunk, chunk, 4, N_HEADS).transpose(0, 4, 3, 1, 2)


def kernel(x, c, ctx, c_ctx, w_mod, b_mod, norm_ffn1, ffn1_up, ffn1_down, norm_mix, w_in, b_in,
           mlstm_norm, conv_w, conv_b, conv_norm_g, conv_norm_b, w_out, norm_ffn2, ffn2_up,
           ffn2_down, norm_final):
    b, s, d = x.shape
    depth = w_mod.shape[0]
    d_ff = ffn1_down.shape[1]
    nck = d_ff // FF_CHUNK
    tm_c = ctx.shape[1]
    L = MLSTM_CHUNK

    def flat(a):
        return a.reshape(1, -1, a.shape[-1])

    rows = -(-(b + 1) // 8) * 8
    c_all = jnp.concatenate([c, c_ctx[None, :], jnp.zeros((rows - b - 1, d), F32)], axis=0)
    mods_all = _mod_call(c_all, w_mod, b_mod).reshape(depth, rows, N_MOD, 1, d)

    q0, k0, v0, o0 = 0, D_MLSTM, 2 * D_MLSTM, 3 * D_MLSTM
    g0 = 4 * D_MLSTM
    u0 = g0 + 4 * N_HEADS
    n_nat = D_MLSTM + 2 * D_CONV + LANES

    xc = ctx
    for l in range(depth):
        last = l == depth - 1
        mods = mods_all[l]

        def ffn_weights(up, down):
            return (_chunked_cols(up[:, :d_ff], nck).astype(BF16),
                    _chunked_cols(up[:, d_ff:], nck).astype(BF16),
                    down.reshape(nck, FF_CHUNK, d).astype(BF16))

        w1 = ffn_weights(ffn1_up[l], ffn1_down[l])
        x = _ffn_call(x, mods, norm_ffn1[l], *w1, js=0, mod_row=None, tm=FFN_TILE)
        xc = _ffn_call(flat(xc), mods, norm_ffn1[l], *w1, js=0, mod_row=b, tm=FFN_TILE).reshape(xc.shape)

        wi, bi = w_in[l], b_in[l]
        wn = jnp.concatenate([wi[:, k0:v0], wi[:, u0:], wi[:, g0:u0],
                              jnp.zeros((d, LANES - 4 * N_HEADS), F32)], axis=1).astype(BF16)
        bn = jnp.concatenate([bi[k0:v0], bi[u0:], bi[g0:u0],
                              jnp.zeros((LANES - 4 * N_HEADS,), F32)]).reshape(1, n_nat)
        wt = jnp.concatenate([wi[:, q0:k0], wi[:, v0:o0], wi[:, o0:g0]], axis=1).T.astype(BF16)
        bt = jnp.broadcast_to(
            jnp.concatenate([bi[q0:k0], bi[v0:o0], bi[o0:g0]])[:, None], (3 * D_MLSTM, LANES))
        kl, zl, gl, qtl, vtl, otl = _proj_call(x, mods, norm_mix[l], wn, bn, wt, bt,
                                               mod_row=None, tm=PROJ_TILE, chunk=L)
        kc, zc, gc, qtc, vtc, otc = _proj_call(xc, mods, norm_mix[l], wn, bn, wt, bt,
                                               mod_row=b, tm=tm_c, chunk=L)

        gates = jnp.concatenate([_gate_rows(gc, L), _gate_rows(gl, L)], axis=3)
        gain = jnp.broadcast_to(mlstm_norm[l][:, None], (D_MLSTM, LANES))
        ymc, yml = _mlstm_call(kc, qtc, vtc, otc, kl, qtl, vtl, otl, gates, gain,
                               ctx_out=not last)

        half = D_CONV // 2
        z_cols = _conv_call(zl, conv_w[l], ch0=0, nblk=half // LANES, seg=GRID_W)
        z_rows = _conv_call(zl, conv_w[l], ch0=half // LANES, nblk=half // LANES, stride=GRID_W)
        wm = w_out[l][:D_MLSTM].astype(BF16)
        wc = w_out[l][D_MLSTM:].astype(BF16)
        mix = (conv_b[l], conv_norm_g[l], conv_norm_b[l], wm, wc)

        w2 = ffn_weights(ffn2_up[l], ffn2_down[l])
        x = _ffn_call(x, mods, norm_ffn2[l], *w2, js=6, mod_row=None, tm=FFN_TILE,
                      final_g=norm_final if last else None, mix=(yml, [z_cols, z_rows], *mix))
        if not last:
            z_ctx = _conv_call(zc, conv_w[l], ch0=0, nblk=D_CONV // LANES, seg=zc.shape[1])
            xc = _ffn_call(flat(xc), mods, norm_ffn2[l], *w2, js=6, mod_row=b, tm=FFN_TILE,
                           mix=(flat(ymc), [flat(z_ctx)], *mix)).reshape(xc.shape)
    return x
```

```python
import functools

import jax
import jax.numpy as jnp
from jax import lax
from jax.experimental import pallas as pl
from jax.experimental.pallas import tpu as pltpu

F32 = jnp.float32
BF16 = jnp.bfloat16

GRID_W = 64
N_HEADS = 4
HEAD_DIM = 128
D_MLSTM = N_HEADS * HEAD_DIM
D_CONV = 512
CONV_WIDTH = 31
CONV_PAD = (CONV_WIDTH - 1) // 2
N_MOD = 9
EPS = 1e-6

LANES = 128
FF_CHUNK = 256
FFN_TILE = 1024
PROJ_TILE = 1024
MLSTM_CHUNK = 128
FFN_UNROLL = 5
STATE_ROWS = HEAD_DIM + 16
NEG_BIG = -1e30
LOG2E = 1.4426950408889634
VMEM_LIMIT = 56 * 1024 * 1024


def _cparams(n_axes):
    return pltpu.CompilerParams(
        dimension_semantics=("arbitrary",) * n_axes, vmem_limit_bytes=VMEM_LIMIT)


def _resident(shape):
    nd = len(shape)
    return pl.BlockSpec(shape, lambda *_: (0,) * nd, pipeline_mode=pl.Buffered(1))


def _rms_mod(x, g, shift, scale):
    ms = jnp.mean(x * x, axis=-1, keepdims=True)
    return (x * lax.rsqrt(ms + EPS)) * (g * (1.0 + scale)) + shift


def _mod_kernel(c_ref, w_ref, b_ref, o_ref):
    c = c_ref[...]
    s = (c * jax.nn.sigmoid(c)).astype(BF16)
    o_ref[0] = jnp.dot(s, w_ref[0].astype(BF16), preferred_element_type=F32) + b_ref[0]


def _mod_call(c_all, w_mod, b_mod):
    depth, d, n = w_mod.shape
    r = c_all.shape[0]
    tn = n // 8
    return pl.pallas_call(
        _mod_kernel,
        grid=(depth, n // tn),
        in_specs=[pl.BlockSpec((r, d), lambda l, j: (0, 0)),
                  pl.BlockSpec((1, d, tn), lambda l, j: (l, 0, j)),
                  pl.BlockSpec((1, 1, tn), lambda l, j: (l, 0, j))],
        out_specs=pl.BlockSpec((1, r, tn), lambda l, j: (l, 0, j)),
        out_shape=jax.ShapeDtypeStruct((depth, r, n), F32),
        compiler_params=_cparams(2),
        name="mod",
    )(c_all, w_mod, b_mod.reshape(depth, 1, n))


def _ffn_kernel(*refs, js, final, nz):
    x_ref, mod_ref = refs[:2]
    refs = refs[2:]
    if nz:
        ym_ref, z_refs = refs[0], refs[1:1 + nz]
        cb_ref, cg_ref, cnb_ref, wm_ref, wc_ref = refs[1 + nz:6 + nz]
        refs = refs[6 + nz:]
    ng_ref, wg_ref, wu_ref, wd_ref = refs[:4]
    refs = refs[4:]
    if final:
        gf_ref, refs = refs[0], refs[1:]
    o_ref, h_ref, a_ref, acc_ref = refs
    nck = wg_ref.shape[0]

    x = x_ref[0]
    if nz:
        z = jnp.concatenate([r[0] for r in z_refs], axis=-1) if nz > 1 else z_refs[0][0]
        z = z + cb_ref[...]
        zc = z - jnp.mean(z, axis=-1, keepdims=True)
        var = jnp.mean(zc * zc, axis=-1, keepdims=True)
        ln = zc * lax.rsqrt(var + EPS) * cg_ref[...] + cnb_ref[...]
        yc = (ln * jax.nn.sigmoid(ln)).astype(BF16)
        y = (jnp.dot(ym_ref[0], wm_ref[...], preferred_element_type=F32)
             + jnp.dot(yc, wc_ref[...], preferred_element_type=F32))
        x = x + mod_ref[0, 5] * y
    o_ref[0] = x
    h_ref[...] = _rms_mod(x, ng_ref[...], mod_ref[0, js], mod_ref[0, js + 1]).astype(BF16)

    def up(c):
        h = h_ref[...]
        g = jnp.dot(h, wg_ref[c], preferred_element_type=F32)
        u = jnp.dot(h, wu_ref[c], preferred_element_type=F32)
        return (g * jax.nn.sigmoid(g) * u).astype(BF16)

    a_ref[0] = up(0)
    acc_ref[...] = jnp.zeros_like(acc_ref)

    def chunk(c, carry):
        a_prev = a_ref[(c - 1) % 2]
        a_ref[c % 2] = up(c)
        acc_ref[...] += jnp.dot(a_prev, wd_ref[c - 1], preferred_element_type=F32)
        return carry

    lax.fori_loop(1, nck, chunk, 0, unroll=FFN_UNROLL)
    d_last = jnp.dot(a_ref[(nck - 1) % 2], wd_ref[nck - 1], preferred_element_type=F32)
    out = o_ref[0] + (0.5 * mod_ref[0, js + 2]) * (acc_ref[...] + d_last)
    if final:
        ms = jnp.mean(out * out, axis=-1, keepdims=True)
        out = out * lax.rsqrt(ms + EPS) * gf_ref[...]
    o_ref[0] = out


def _ffn_call(x, mods, ng, wg, wu, wd, *, js, mod_row, tm, final_g=None, mix=None):
    b, t, d = x.shape
    nck, _, fc = wg.shape
    row = (lambda i, j: (i, 0, 0, 0)) if mod_row is None else (lambda i, j: (mod_row, 0, 0, 0))

    def tok(c):
        return pl.BlockSpec((1, tm, c), lambda i, j: (i, j, 0))

    in_specs = [tok(d), pl.BlockSpec((1, N_MOD, 1, d), row)]
    args = [x, mods]
    nz = 0
    if mix is not None:
        ym, zs, cb, cg, cnb, wm, wc = mix
        nz = len(zs)
        in_specs += [tok(ym.shape[2])] + [tok(z.shape[2]) for z in zs]
        in_specs += [_resident((1, D_CONV))] * 3 + [_resident(wm.shape), _resident(wc.shape)]
        args += [ym, *zs, cb.reshape(1, -1), cg.reshape(1, -1), cnb.reshape(1, -1), wm, wc]
    in_specs += [_resident((1, d)), _resident(wg.shape), _resident(wu.shape), _resident(wd.shape)]
    args += [ng.reshape(1, d), wg, wu, wd]
    if final_g is not None:
        in_specs.append(_resident((1, d)))
        args.append(final_g.reshape(1, d))
    return pl.pallas_call(
        functools.partial(_ffn_kernel, js=js, final=final_g is not None, nz=nz),
        grid=(b, t // tm),
        in_specs=in_specs,
        out_specs=tok(d),
        out_shape=jax.ShapeDtypeStruct((b, t, d), F32),
        scratch_shapes=[pltpu.VMEM((tm, d), BF16), pltpu.VMEM((2, tm, fc), BF16),
                        pltpu.VMEM((tm, d), F32)],
        compiler_params=_cparams(2),
        name="ffn",
    )(*args)


def _proj_kernel(x_ref, mod_ref, ng_ref, wn_ref, bn_ref, wt_ref, bt_ref, gain_ref,
                 k_ref, z_ref, g_ref, qt_ref, vt_ref, ot_ref, *, chunk):
    tm = x_ref.shape[1]
    u = _rms_mod(x_ref[0], ng_ref[...], mod_ref[0, 3], mod_ref[0, 4]).astype(BF16)
    pn = jnp.dot(u, wn_ref[...], preferred_element_type=F32) + bn_ref[...]
    for h in range(N_HEADS):
        k_ref[0, h] = pn[:, h * HEAD_DIM:(h + 1) * HEAD_DIM].astype(BF16)
    a = pn[:, D_MLSTM:D_MLSTM + D_CONV]
    gl = pn[:, D_MLSTM + D_CONV:D_MLSTM + 2 * D_CONV]
    z_ref[0] = a * jax.nn.sigmoid(gl)
    pt = lax.dot_general(wt_ref[...], u, (((1,), (1,)), ((), ())), preferred_element_type=F32)
    rg = slice(3 * D_MLSTM, 3 * D_MLSTM + 4 * N_HEADS)
    for lb in range(tm // LANES):
        cols = slice(lb * LANES, (lb + 1) * LANES)
        g_ref[0, :, cols] = pt[rg, cols] + bt_ref[rg, :]
    scale = HEAD_DIM ** -0.5
    for h in range(N_HEADS):
        for lb in range(tm // LANES):
            c, off = divmod(lb * LANES, chunk)
            cols = slice(lb * LANES, (lb + 1) * LANES)
            dst = (0, h, c, slice(None), slice(off, off + LANES))
            rq = slice(h * HEAD_DIM, (h + 1) * HEAD_DIM)
            rv = slice(D_MLSTM + h * HEAD_DIM, D_MLSTM + (h + 1) * HEAD_DIM)
            ro = slice(2 * D_MLSTM + h * HEAD_DIM, 2 * D_MLSTM + (h + 1) * HEAD_DIM)
            qt_ref[dst] = ((pt[rq, cols] + bt_ref[rq, :]) * scale).astype(BF16)
            vt_ref[dst] = (pt[rv, cols] + bt_ref[rv, :]).astype(BF16)
            ot_ref[dst] = jax.nn.sigmoid(pt[ro, cols] + bt_ref[ro, :]) * gain_ref[rq, :]


def _proj_call(x, mods, ng, wn, bn, wt, bt, gain, *, mod_row, tm, chunk):
    b, t, d = x.shape
    nch, cpt = t // chunk, tm // chunk
    row = (lambda i, j: (i, 0, 0, 0)) if mod_row is None else (lambda i, j: (mod_row, 0, 0, 0))
    tspec = pl.BlockSpec((1, N_HEADS, cpt, HEAD_DIM, chunk), lambda i, j: (i, 0, j, 0, 0))
    tshape = (b, N_HEADS, nch, HEAD_DIM, chunk)
    return pl.pallas_call(
        functools.partial(_proj_kernel, chunk=chunk),
        grid=(b, t // tm),
        in_specs=[pl.BlockSpec((1, tm, d), lambda i, j: (i, j, 0)),
                  pl.BlockSpec((1, N_MOD, 1, d), row),
                  _resident((1, d)), _resident(wn.shape), _resident(bn.shape),
                  _resident(wt.shape), _resident(bt.shape), _resident(gain.shape)],
        out_specs=[pl.BlockSpec((1, N_HEADS, tm, HEAD_DIM), lambda i, j: (i, 0, j, 0)),
                   pl.BlockSpec((1, tm, D_CONV), lambda i, j: (i, j, 0)),
                   pl.BlockSpec((1, 4 * N_HEADS, tm), lambda i, j: (i, 0, j)),
                   tspec, tspec, tspec],
        out_shape=[jax.ShapeDtypeStruct((b, N_HEADS, t, HEAD_DIM), BF16),
                   jax.ShapeDtypeStruct((b, t, D_CONV), F32),
                   jax.ShapeDtypeStruct((b, 4 * N_HEADS, t), F32),
                   jax.ShapeDtypeStruct(tshape, BF16),
                   jax.ShapeDtypeStruct(tshape, BF16),
                   jax.ShapeDtypeStruct(tshape, F32)],
        compiler_params=_cparams(2),
        name="proj",
    )(x, mods, ng.reshape(1, d), wn, bn, wt, bt, gain)


def _scan_max(r, reverse):
    n = r.shape[-1]
    lane = lax.broadcasted_iota(jnp.int32, r.shape, r.ndim - 1)
    s = 1
    while s < n:
        if reverse:
            sh = jnp.where(lane < n - s, pltpu.roll(r, n - s, r.ndim - 1), NEG_BIG)
        else:
            sh = jnp.where(lane >= s, pltpu.roll(r, s, r.ndim - 1), NEG_BIG)
        r = jnp.maximum(r, sh)
        s *= 2
    return r


def _mlstm_kernel(kc_ref, qtc_ref, vtc_ref, otc_ref, kl_ref, qtl_ref, vtl_ref, otl_ref,
                  g_ref, yc_ref, yl_ref, gs_ref, rt_ref, upd_ref, st_ref, *, ctx_out):
    n_c, n_l = qtc_ref.shape[2], qtl_ref.shape[2]
    n_ch = n_c + n_l
    L = qtl_ref.shape[4]
    g = g_ref[0, 0]
    srow = lax.broadcasted_iota(jnp.int32, (L, L), 0)
    tcol = lax.broadcasted_iota(jnp.int32, (L, L), 1)
    masks = (srow <= tcol, srow >= tcol)
    orders = (list(range(n_ch)),
              list(range(n_c - 1, -1, -1)) + list(range(n_ch - 1, n_c - 1, -1)))

    for d in range(2):
        li, fpre = g[2 * d], g[2 * d + 1]
        lf = jnp.minimum(fpre, 0.0) - jnp.log1p(jnp.exp(-jnp.abs(fpre)))
        tri = masks[d].astype(F32)
        cum = jnp.dot(lf, tri, preferred_element_type=F32, precision=lax.Precision.HIGHEST)
        edge = L - 1 if d == 0 else 0
        total = cum[:, edge:edge + 1]
        r = li - cum
        pm = _scan_max(r, reverse=(d == 1))
        rmax = pm[:, edge:edge + 1]
        m = jnp.zeros((1, 1), F32)
        m_prev, m_new = [None] * n_ch, [None] * n_ch
        for c in orders[d]:
            m_prev[c] = m
            m = total[c:c + 1] + jnp.maximum(m, rmax[c:c + 1])
            m_new[c] = m
        m_prev = jnp.concatenate(m_prev, axis=0)
        m_new = jnp.concatenate(m_new, axis=0)
        mx = jnp.maximum(m_prev, pm)
        alpha = -LOG2E * mx
        inter = jnp.exp(m_prev - mx)
        clampv = jnp.exp(-(cum + mx))
        wk = jnp.exp(total + r - m_new)
        decay = jnp.broadcast_to(jnp.exp(total + m_prev - m_new), (n_ch, L))
        rt_ref[d] = jnp.concatenate([LOG2E * r, jnp.zeros((LANES - n_ch, L), F32)], axis=0).T
        pad = jnp.zeros((3, L), F32)
        for c in range(n_ch):
            gs_ref[d, c] = jnp.concatenate(
                [alpha[c:c + 1], inter[c:c + 1], clampv[c:c + 1], wk[c:c + 1], decay[c:c + 1], pad],
                axis=0)

    def increments(cidx, k, vt):
        vtf = vt.astype(F32)
        parts = []
        for d in range(2):
            wk = gs_ref[d, cidx][3:4]
            parts += [vtf * wk, wk, jnp.zeros((STATE_ROWS - HEAD_DIM - 1, L), F32)]
        lhs = jnp.concatenate(parts, axis=0).astype(BF16)
        upd = jnp.dot(lhs, k, preferred_element_type=F32)
        upd_ref[0, cidx] = upd[:STATE_ROWS]
        upd_ref[1, cidx] = upd[STATE_ROWS:]

    for c in range(n_c):
        increments(c, kc_ref[0, 0, c * L:(c + 1) * L, :], vtc_ref[0, 0, c])
    for c in range(n_l):
        increments(n_c + c, kl_ref[0, 0, c * L:(c + 1) * L, :], vtl_ref[0, 0, c])

    for d in range(2):
        state = jnp.zeros((STATE_ROWS, HEAD_DIM), F32)
        for c in orders[d]:
            st_ref[d, c] = state.astype(BF16)
            state = gs_ref[d, c][4:5, :HEAD_DIM] * state + upd_ref[d, c]

    def outputs(cidx, k, qt, vt, og):
        s_t = jnp.dot(k, qt, preferred_element_type=F32)
        carried = jnp.dot(jnp.concatenate([st_ref[0, cidx], st_ref[1, cidx]], axis=0), qt,
                          preferred_element_type=F32)
        sws, dens, rows = [], [], []
        for d in range(2):
            gsl = gs_ref[d, cidx]
            rows.append(gsl)
            arg = jnp.where(masks[d], rt_ref[d, :, cidx:cidx + 1] + gsl[0:1], NEG_BIG)
            sw = s_t * jnp.exp2(arg)
            sws.append(sw.astype(BF16))
            qn = carried[d * STATE_ROWS + HEAD_DIM:d * STATE_ROWS + HEAD_DIM + 1]
            dens.append(jnp.sum(sw, axis=0, keepdims=True) + gsl[1:2] * qn)
        num2 = jnp.dot(vt, jnp.concatenate(sws, axis=1), preferred_element_type=F32)
        hsum = None
        for d in range(2):
            num = (num2[:, d * L:(d + 1) * L]
                   + rows[d][1:2] * carried[d * STATE_ROWS:d * STATE_ROWS + HEAD_DIM])
            h = num * (1.0 / jnp.maximum(jnp.abs(dens[d]), rows[d][2:3]))
            hsum = h if hsum is None else hsum + h
        ms = jnp.mean(hsum * hsum, axis=0, keepdims=True)
        y = hsum * lax.rsqrt(ms + EPS) * og
        return y.T.astype(BF16)

    if ctx_out:
        for c in range(n_c):
            yc_ref[0, c * L:(c + 1) * L, :] = outputs(
                c, kc_ref[0, 0, c * L:(c + 1) * L, :], qtc_ref[0, 0, c], vtc_ref[0, 0, c],
                otc_ref[0, 0, c])
    else:
        yc_ref[...] = jnp.zeros_like(yc_ref)

    for c in range(n_l):
        yl_ref[0, c * L:(c + 1) * L, :] = outputs(
            n_c + c, kl_ref[0, 0, c * L:(c + 1) * L, :], qtl_ref[0, 0, c], vtl_ref[0, 0, c],
            otl_ref[0, 0, c])


def _mlstm_call(kc, qtc, vtc, otc, kl, qtl, vtl, otl, gates, *, ctx_out):
    b = kl.shape[0]
    tc, tl = kc.shape[2], kl.shape[2]
    n_c, n_l, L = qtc.shape[2], qtl.shape[2], qtl.shape[4]
    n_ch = n_c + n_l
    assert L == LANES and n_ch <= L

    def nat(t):
        return pl.BlockSpec((1, 1, t, HEAD_DIM), lambda i, h: (i, h, 0, 0))

    def tr(n):
        return pl.BlockSpec((1, 1, n, HEAD_DIM, L), lambda i, h: (i, h, 0, 0, 0))

    return pl.pallas_call(
        functools.partial(_mlstm_kernel, ctx_out=ctx_out),
        grid=(b, N_HEADS),
        in_specs=[nat(tc), tr(n_c), tr(n_c), tr(n_c), nat(tl), tr(n_l), tr(n_l), tr(n_l),
                  pl.BlockSpec((1, 1, 4, n_ch, L), lambda i, h: (i, h, 0, 0, 0))],
        out_specs=[pl.BlockSpec((1, tc, HEAD_DIM), lambda i, h: (i, 0, h)),
                   pl.BlockSpec((1, tl, HEAD_DIM), lambda i, h: (i, 0, h))],
        out_shape=[jax.ShapeDtypeStruct((b, tc, D_MLSTM), BF16),
                   jax.ShapeDtypeStruct((b, tl, D_MLSTM), BF16)],
        scratch_shapes=[pltpu.VMEM((2, n_ch, 8, L), F32),
                        pltpu.VMEM((2, L, LANES), F32),
                        pltpu.VMEM((2, n_ch, STATE_ROWS, HEAD_DIM), F32),
                        pltpu.VMEM((2, n_ch, STATE_ROWS, HEAD_DIM), BF16)],
        compiler_params=_cparams(2),
        name="mlstm",
    )(kc, qtc, vtc, otc, kl, qtl, vtl, otl, gates)


SEG_LPAD = 16
ROW_TILE = 64


def _conv_seg_block(z_ref, w_ref, o_ref, zp_ref, cols, seg):
    t = z_ref.shape[1]
    pitch = seg + 2 * SEG_LPAD
    n_seg = t // seg
    zp_ref[0:n_seg * pitch, :] = jnp.zeros((n_seg * pitch, LANES), F32)
    for s in range(n_seg):
        zp_ref[s * pitch + SEG_LPAD:s * pitch + SEG_LPAD + seg, :] = z_ref[0, s * seg:(s + 1) * seg, cols]
    tiles = seg // ROW_TILE
    wrows = [w_ref[j:j + 1, cols] for j in range(CONV_WIDTH)]

    def tile(i, carry):
        s, q = i // tiles, i % tiles
        src = s * pitch + q * ROW_TILE + (SEG_LPAD - CONV_PAD)
        acc = wrows[0] * zp_ref[pl.ds(src, ROW_TILE), :]
        for j in range(1, CONV_WIDTH):
            acc = acc + wrows[j] * zp_ref[pl.ds(src + j, ROW_TILE), :]
        o_ref[0, pl.ds(pl.multiple_of(i * ROW_TILE, ROW_TILE), ROW_TILE), cols] = acc
        return carry

    lax.fori_loop(0, t // ROW_TILE, tile, 0)


def _conv_rows_block(z_ref, w_ref, o_ref, zp_ref, cols):
    t = z_ref.shape[1]
    pad = CONV_PAD * ROW_TILE
    zp_ref[0:pad, :] = jnp.zeros((pad, LANES), F32)
    zp_ref[pad + t:pad + t + pad, :] = jnp.zeros((pad, LANES), F32)
    zp_ref[pad:pad + t, :] = z_ref[0, :, cols]
    wrows = [w_ref[j:j + 1, cols] for j in range(CONV_WIDTH)]

    def tile(i, carry):
        base = pl.multiple_of(i * ROW_TILE, ROW_TILE)
        acc = wrows[0] * zp_ref[pl.ds(base, ROW_TILE), :]
        for j in range(1, CONV_WIDTH):
            acc = acc + wrows[j] * zp_ref[pl.ds(base + j * ROW_TILE, ROW_TILE), :]
        o_ref[0, pl.ds(base, ROW_TILE), cols] = acc
        return carry

    lax.fori_loop(0, t // ROW_TILE, tile, 0)


def _conv_kernel(z_ref, w_ref, o_ref, zp_ref, *, plan):
    for blk, seg in enumerate(plan):
        cols = slice(blk * LANES, (blk + 1) * LANES)
        if seg is None:
            _conv_rows_block(z_ref, w_ref, o_ref, zp_ref, cols)
        else:
            _conv_seg_block(z_ref, w_ref, o_ref, zp_ref, cols, seg)


def _conv_call(z, w, plan):
    b, t, ch = z.shape
    assert ROW_TILE == GRID_W and len(plan) * LANES == ch
    rows = max((t + 2 * CONV_PAD * ROW_TILE) if seg is None else (t // seg) * (seg + 2 * SEG_LPAD)
               for seg in plan)
    return pl.pallas_call(
        functools.partial(_conv_kernel, plan=plan),
        grid=(b,),
        in_specs=[pl.BlockSpec((1, t, ch), lambda i: (i, 0, 0)),
                  pl.BlockSpec((CONV_WIDTH, ch), lambda i: (0, 0))],
        out_specs=pl.BlockSpec((1, t, ch), lambda i: (i, 0, 0)),
        out_shape=jax.ShapeDtypeStruct((b, t, ch), F32),
        scratch_shapes=[pltpu.VMEM((rows, LANES), F32)],
        compiler_params=_cparams(1),
        name="conv",
    )(z, w)


def _chunked_cols(w, n):
    k = w.shape[0]
    return w.reshape(k, n, FF_CHUNK).transpose(1, 0, 2)


def _gate_rows(g, chunk):
    b, _, t = g.shape
    return g.reshape(b, 4, N_HEADS, t // chunk, chunk).transpose(0, 2, 1, 3, 4)


def kernel(x, c, ctx, c_ctx, w_mod, b_mod, norm_ffn1, ffn1_up, ffn1_down, norm_mix, w_in, b_in,
           mlstm_norm, conv_w, conv_b, conv_norm_g, conv_norm_b, w_out, norm_ffn2, ffn2_up,
           ffn2_down, norm_final):
    b, s, d = x.shape
    depth = w_mod.shape[0]
    d_ff = ffn1_down.shape[1]
    nck = d_ff // FF_CHUNK
    tm_c = ctx.shape[1]
    L = MLSTM_CHUNK

    def flat(a):
        return a.reshape(1, -1, a.shape[-1])

    rows = -(-(b + 1) // 8) * 8
    c_all = jnp.concatenate([c, c_ctx[None, :], jnp.zeros((rows - b - 1, d), F32)], axis=0)
    mods_all = _mod_call(c_all, w_mod, b_mod).reshape(depth, rows, N_MOD, 1, d)

    q0, k0, v0, o0 = 0, D_MLSTM, 2 * D_MLSTM, 3 * D_MLSTM
    g0 = 4 * D_MLSTM
    u0 = g0 + 4 * N_HEADS

    xc = ctx
    for l in range(depth):
        last = l == depth - 1
        mods = mods_all[l]

        def ffn_weights(up, down):
            return (_chunked_cols(up[:, :d_ff], nck).astype(BF16),
                    _chunked_cols(up[:, d_ff:], nck).astype(BF16),
                    down.reshape(nck, FF_CHUNK, d).astype(BF16))

        w1 = ffn_weights(ffn1_up[l], ffn1_down[l])
        x = _ffn_call(x, mods, norm_ffn1[l], *w1, js=0, mod_row=None, tm=FFN_TILE)
        xc = _ffn_call(flat(xc), mods, norm_ffn1[l], *w1, js=0, mod_row=b, tm=FFN_TILE).reshape(xc.shape)

        wi, bi = w_in[l], b_in[l]
        wn = jnp.concatenate([wi[:, k0:v0], wi[:, u0:]], axis=1).astype(BF16)
        bn = jnp.concatenate([bi[k0:v0], bi[u0:]]).reshape(1, -1)
        wt = jnp.concatenate([wi[:, q0:k0], wi[:, v0:u0]], axis=1).T.astype(BF16)
        bt = jnp.broadcast_to(jnp.concatenate([bi[q0:k0], bi[v0:u0]])[:, None],
                              (wt.shape[0], LANES))
        gain = jnp.broadcast_to(mlstm_norm[l][:, None], (D_MLSTM, LANES))
        kl, zl, gl, qtl, vtl, otl = _proj_call(x, mods, norm_mix[l], wn, bn, wt, bt, gain,
                                               mod_row=None, tm=PROJ_TILE, chunk=L)
        kc, zc, gc, qtc, vtc, otc = _proj_call(xc, mods, norm_mix[l], wn, bn, wt, bt, gain,
                                               mod_row=b, tm=tm_c, chunk=L)

        gates = jnp.concatenate([_gate_rows(gc, L), _gate_rows(gl, L)], axis=3)
        ymc, yml = _mlstm_call(kc, qtc, vtc, otc, kl, qtl, vtl, otl, gates, ctx_out=not last)

        nblk = D_CONV // LANES
        z_lat = _conv_call(zl, conv_w[l], (GRID_W,) * (nblk // 2) + (None,) * (nblk - nblk // 2))
        wm = w_out[l][:D_MLSTM].astype(BF16)
        wc = w_out[l][D_MLSTM:].astype(BF16)
        mix = (conv_b[l], conv_norm_g[l], conv_norm_b[l], wm, wc)

        w2 = ffn_weights(ffn2_up[l], ffn2_down[l])
        x = _ffn_call(x, mods, norm_ffn2[l], *w2, js=6, mod_row=None, tm=FFN_TILE,
                      final_g=norm_final if last else None, mix=(yml, [z_lat], *mix))
        if not last:
            z_ctx = _conv_call(zc, conv_w[l], (zc.shape[1],) * nblk)
            xc = _ffn_call(flat(xc), mods, norm_ffn2[l], *w2, js=6, mod_row=b, tm=FFN_TILE,
                           mix=(flat(ymc), [flat(z_ctx)], *mix)).reshape(xc.shape)
    return x
```

```python
import functools

import jax
import jax.numpy as jnp
from jax import lax
from jax.experimental import pallas as pl
from jax.experimental.pallas import tpu as pltpu

F32 = jnp.float32
BF16 = jnp.bfloat16

GRID_W = 64
N_HEADS = 4
HEAD_DIM = 128
D_MLSTM = N_HEADS * HEAD_DIM
D_CONV = 512
CONV_WIDTH = 31
CONV_PAD = (CONV_WIDTH - 1) // 2
N_MOD = 9
EPS = 1e-6

LANES = 128
FF_CHUNK = 256
FFN_TILE = 1024
PROJ_TILE = 1024
MLSTM_CHUNK = 128
FFN_UNROLL = 5
STATE_ROWS = HEAD_DIM + 16
NEG_BIG = -1e30
LOG2E = 1.4426950408889634
VMEM_LIMIT = 56 * 1024 * 1024


def _cparams(n_axes):
    return pltpu.CompilerParams(
        dimension_semantics=("arbitrary",) * n_axes, vmem_limit_bytes=VMEM_LIMIT)


def _resident(shape):
    nd = len(shape)
    return pl.BlockSpec(shape, lambda *_: (0,) * nd, pipeline_mode=pl.Buffered(1))


def _rms_mod(x, g, shift, scale):
    ms = jnp.mean(x * x, axis=-1, keepdims=True)
    return (x * lax.rsqrt(ms + EPS)) * (g * (1.0 + scale)) + shift


def _mod_kernel(c_ref, w_ref, b_ref, o_ref):
    c = c_ref[...]
    s = (c * jax.nn.sigmoid(c)).astype(BF16)
    o_ref[0] = jnp.dot(s, w_ref[0].astype(BF16), preferred_element_type=F32) + b_ref[0]


def _mod_call(c_all, w_mod, b_mod):
    depth, d, n = w_mod.shape
    r = c_all.shape[0]
    tn = n // 8
    return pl.pallas_call(
        _mod_kernel,
        grid=(depth, n // tn),
        in_specs=[pl.BlockSpec((r, d), lambda l, j: (0, 0)),
                  pl.BlockSpec((1, d, tn), lambda l, j: (l, 0, j)),
                  pl.BlockSpec((1, 1, tn), lambda l, j: (l, 0, j))],
        out_specs=pl.BlockSpec((1, r, tn), lambda l, j: (l, 0, j)),
        out_shape=jax.ShapeDtypeStruct((depth, r, n), F32),
        compiler_params=_cparams(2),
        name="mod",
    )(c_all, w_mod, b_mod.reshape(depth, 1, n))


def _ffn_kernel(*refs, js, final, nz):
    x_ref, mod_ref = refs[:2]
    refs = refs[2:]
    if nz:
        ym_ref, z_refs = refs[0], refs[1:1 + nz]
        cb_ref, cg_ref, cnb_ref, wm_ref, wc_ref = refs[1 + nz:6 + nz]
        refs = refs[6 + nz:]
    ng_ref, wup_ref, wd_ref = refs[:3]
    refs = refs[3:]
    if final:
        gf_ref, refs = refs[0], refs[1:]
    o_ref, h_ref, a_ref, acc_ref = refs
    nck, fc, _ = wd_ref.shape
    d_ff = nck * fc

    x = x_ref[0]
    if nz:
        z = jnp.concatenate([r[0] for r in z_refs], axis=-1) if nz > 1 else z_refs[0][0]
        z = z + cb_ref[...]
        zc = z - jnp.mean(z, axis=-1, keepdims=True)
        var = jnp.mean(zc * zc, axis=-1, keepdims=True)
        ln = zc * lax.rsqrt(var + EPS) * cg_ref[...] + cnb_ref[...]
        yc = (ln * jax.nn.sigmoid(ln)).astype(BF16)
        y = (jnp.dot(ym_ref[0], wm_ref[...], preferred_element_type=F32)
             + jnp.dot(yc, wc_ref[...], preferred_element_type=F32))
        x = x + mod_ref[0, 5] * y
    o_ref[0] = x
    h_ref[...] = _rms_mod(x, ng_ref[...], mod_ref[0, js], mod_ref[0, js + 1]).astype(BF16)

    def up(c):
        h = h_ref[...]
        col = c * fc if isinstance(c, int) else pl.multiple_of(c * fc, fc)
        g = jnp.dot(h, wup_ref[:, pl.ds(col, fc)], preferred_element_type=F32)
        u = jnp.dot(h, wup_ref[:, pl.ds(d_ff + col, fc)], preferred_element_type=F32)
        return (g * jax.nn.sigmoid(g) * u).astype(BF16)

    a_ref[0] = up(0)
    acc_ref[...] = jnp.zeros_like(acc_ref)

    def chunk(c, carry):
        a_prev = a_ref[(c - 1) % 2]
        a_ref[c % 2] = up(c)
        acc_ref[...] += jnp.dot(a_prev, wd_ref[c - 1], preferred_element_type=F32)
        return carry

    lax.fori_loop(1, nck, chunk, 0, unroll=FFN_UNROLL)
    d_last = jnp.dot(a_ref[(nck - 1) % 2], wd_ref[nck - 1], preferred_element_type=F32)
    out = o_ref[0] + (0.5 * mod_ref[0, js + 2]) * (acc_ref[...] + d_last)
    if final:
        ms = jnp.mean(out * out, axis=-1, keepdims=True)
        out = out * lax.rsqrt(ms + EPS) * gf_ref[...]
    o_ref[0] = out


def _ffn_call(x, mods, ng, wup, wd, *, js, mod_row, tm, final_g=None, mix=None):
    b, t, d = x.shape
    fc = wd.shape[1]
    row = (lambda i, j: (i, 0, 0, 0)) if mod_row is None else (lambda i, j: (mod_row, 0, 0, 0))

    def tok(c):
        return pl.BlockSpec((1, tm, c), lambda i, j: (i, j, 0))

    in_specs = [tok(d), pl.BlockSpec((1, N_MOD, 1, d), row)]
    args = [x, mods]
    nz = 0
    if mix is not None:
        ym, zs, cb, cg, cnb, wm, wc = mix
        nz = len(zs)
        in_specs += [tok(ym.shape[2])] + [tok(z.shape[2]) for z in zs]
        in_specs += [_resident((1, D_CONV))] * 3 + [_resident(wm.shape), _resident(wc.shape)]
        args += [ym, *zs, cb.reshape(1, -1), cg.reshape(1, -1), cnb.reshape(1, -1), wm, wc]
    in_specs += [_resident((1, d)), _resident(wup.shape), _resident(wd.shape)]
    args += [ng.reshape(1, d), wup, wd]
    if final_g is not None:
        in_specs.append(_resident((1, d)))
        args.append(final_g.reshape(1, d))
    return pl.pallas_call(
        functools.partial(_ffn_kernel, js=js, final=final_g is not None, nz=nz),
        grid=(b, t // tm),
        in_specs=in_specs,
        out_specs=tok(d),
        out_shape=jax.ShapeDtypeStruct((b, t, d), F32),
        scratch_shapes=[pltpu.VMEM((tm, d), BF16), pltpu.VMEM((2, tm, fc), BF16),
                        pltpu.VMEM((tm, d), F32)],
        compiler_params=_cparams(2),
        name="ffn",
    )(*args)


def _proj_kernel(x_ref, mod_ref, ng_ref, wn_ref, bn_ref, wt_ref, bt_ref, gain_ref,
                 k_ref, z_ref, g_ref, qt_ref, vt_ref, ot_ref, *, chunk):
    tm = x_ref.shape[1]
    u = _rms_mod(x_ref[0], ng_ref[...], mod_ref[0, 3], mod_ref[0, 4]).astype(BF16)
    pn = jnp.dot(u, wn_ref[...], preferred_element_type=F32) + bn_ref[...]
    for h in range(N_HEADS):
        k_ref[0, h] = pn[:, h * HEAD_DIM:(h + 1) * HEAD_DIM].astype(BF16)
    a = pn[:, D_MLSTM:D_MLSTM + D_CONV]
    gl = pn[:, D_MLSTM + D_CONV:D_MLSTM + 2 * D_CONV]
    z_ref[0] = a * jax.nn.sigmoid(gl)
    pt = lax.dot_general(wt_ref[...], u, (((1,), (1,)), ((), ())), preferred_element_type=F32)
    rg = slice(3 * D_MLSTM, 3 * D_MLSTM + 4 * N_HEADS)
    for lb in range(tm // LANES):
        cols = slice(lb * LANES, (lb + 1) * LANES)
        g_ref[0, :, cols] = pt[rg, cols] + bt_ref[rg, :]
    scale = HEAD_DIM ** -0.5
    for h in range(N_HEADS):
        for lb in range(tm // LANES):
            c, off = divmod(lb * LANES, chunk)
            cols = slice(lb * LANES, (lb + 1) * LANES)
            dst = (0, h, c, slice(None), slice(off, off + LANES))
            rq = slice(h * HEAD_DIM, (h + 1) * HEAD_DIM)
            rv = slice(D_MLSTM + h * HEAD_DIM, D_MLSTM + (h + 1) * HEAD_DIM)
            ro = slice(2 * D_MLSTM + h * HEAD_DIM, 2 * D_MLSTM + (h + 1) * HEAD_DIM)
            qt_ref[dst] = ((pt[rq, cols] + bt_ref[rq, :]) * scale).astype(BF16)
            vt_ref[dst] = (pt[rv, cols] + bt_ref[rv, :]).astype(BF16)
            ot_ref[dst] = jax.nn.sigmoid(pt[ro, cols] + bt_ref[ro, :]) * gain_ref[rq, :]


def _proj_call(x, mods, ng, wn, bn, wt, bt, gain, *, mod_row, tm, chunk):
    b, t, d = x.shape
    nch, cpt = t // chunk, tm // chunk
    row = (lambda i, j: (i, 0, 0, 0)) if mod_row is None else (lambda i, j: (mod_row, 0, 0, 0))
    tspec = pl.BlockSpec((1, N_HEADS, cpt, HEAD_DIM, chunk), lambda i, j: (i, 0, j, 0, 0))
    tshape = (b, N_HEADS, nch, HEAD_DIM, chunk)
    return pl.pallas_call(
        functools.partial(_proj_kernel, chunk=chunk),
        grid=(b, t // tm),
        in_specs=[pl.BlockSpec((1, tm, d), lambda i, j: (i, j, 0)),
                  pl.BlockSpec((1, N_MOD, 1, d), row),
                  _resident((1, d)), _resident(wn.shape), _resident(bn.shape),
                  _resident(wt.shape), _resident(bt.shape), _resident(gain.shape)],
        out_specs=[pl.BlockSpec((1, N_HEADS, tm, HEAD_DIM), lambda i, j: (i, 0, j, 0)),
                   pl.BlockSpec((1, tm, D_CONV), lambda i, j: (i, j, 0)),
                   pl.BlockSpec((1, 4 * N_HEADS, tm), lambda i, j: (i, 0, j)),
                   tspec, tspec, tspec],
        out_shape=[jax.ShapeDtypeStruct((b, N_HEADS, t, HEAD_DIM), BF16),
                   jax.ShapeDtypeStruct((b, t, D_CONV), F32),
                   jax.ShapeDtypeStruct((b, 4 * N_HEADS, t), F32),
                   jax.ShapeDtypeStruct(tshape, BF16),
                   jax.ShapeDtypeStruct(tshape, BF16),
                   jax.ShapeDtypeStruct(tshape, F32)],
        compiler_params=_cparams(2),
        name="proj",
    )(x, mods, ng.reshape(1, d), wn, bn, wt, bt, gain)


def _mlstm_state_pass(g_ref, kc_ref, vtc_ref, kl_ref, vtl_ref, hh, scratch, masks, orders):
    gs_ref, rt_ref, upd_ref, st_ref = scratch
    _, n_ch, L = g_ref.shape[2:]
    for d in range(2):
        li, fpre = g_ref[0, hh, 2 * d], g_ref[0, hh, 2 * d + 1]
        lf = jnp.minimum(fpre, 0.0) - jnp.log1p(jnp.exp(-jnp.abs(fpre)))
        tri = masks[d].astype(F32)
        cum = jnp.dot(lf, tri, preferred_element_type=F32, precision=lax.Precision.HIGHEST)
        edge = L - 1 if d == 0 else 0
        total = cum[:, edge:edge + 1]
        r = li - cum
        yield
        pm = r
        lane = lax.broadcasted_iota(jnp.int32, r.shape, 1)
        s = 1
        while s < L:
            if d == 1:
                sh = jnp.where(lane < L - s, pltpu.roll(pm, L - s, 1), NEG_BIG)
            else:
                sh = jnp.where(lane >= s, pltpu.roll(pm, s, 1), NEG_BIG)
            pm = jnp.maximum(pm, sh)
            s *= 2
            yield
        rmax = pm[:, edge:edge + 1]
        m = jnp.zeros((1, 1), F32)
        m_prev, m_new = [None] * n_ch, [None] * n_ch
        for c in orders[d]:
            m_prev[c] = m
            m = total[c:c + 1] + jnp.maximum(m, rmax[c:c + 1])
            m_new[c] = m
        yield
        m_prev = jnp.concatenate(m_prev, axis=0)
        m_new = jnp.concatenate(m_new, axis=0)
        mx = jnp.maximum(m_prev, pm)
        alpha = -LOG2E * mx
        inter = jnp.exp(m_prev - mx)
        clampv = jnp.exp(-(cum + mx))
        wk = jnp.exp(total + r - m_new)
        decay = jnp.broadcast_to(jnp.exp(total + m_prev - m_new), (n_ch, L))
        rt_ref[d] = jnp.concatenate([LOG2E * r, jnp.zeros((LANES - n_ch, L), F32)], axis=0).T
        pad = jnp.zeros((3, L), F32)
        for c in range(n_ch):
            gs_ref[d, c] = jnp.concatenate(
                [alpha[c:c + 1], inter[c:c + 1], clampv[c:c + 1], wk[c:c + 1], decay[c:c + 1], pad],
                axis=0)
        yield

    n_c = kc_ref.shape[2] // L
    for c in range(n_ch):
        if c < n_c:
            k, vt = kc_ref[0, hh, c * L:(c + 1) * L, :], vtc_ref[0, hh, c]
        else:
            k, vt = kl_ref[0, hh, (c - n_c) * L:(c - n_c + 1) * L, :], vtl_ref[0, hh, c - n_c]
        vtf = vt.astype(F32)
        parts = []
        for d in range(2):
            wk = gs_ref[d, c][3:4]
            parts += [vtf * wk, wk, jnp.zeros((STATE_ROWS - HEAD_DIM - 1, L), F32)]
        lhs = jnp.concatenate(parts, axis=0).astype(BF16)
        upd = jnp.dot(lhs, k, preferred_element_type=F32)
        upd_ref[0, c] = upd[:STATE_ROWS]
        upd_ref[1, c] = upd[STATE_ROWS:]
        yield

    states = [jnp.zeros((STATE_ROWS, HEAD_DIM), F32)] * 2
    for i in range(n_ch):
        for d in range(2):
            c = orders[d][i]
            st_ref[d, c] = states[d].astype(BF16)
            states[d] = gs_ref[d, c][4:5, :HEAD_DIM] * states[d] + upd_ref[d, c]
        yield


def _mlstm_kernel(kc_ref, qtc_ref, vtc_ref, otc_ref, kl_ref, qtl_ref, vtl_ref, otl_ref, g_ref,
                  kcn_ref, vtcn_ref, kln_ref, vtln_ref, gn_ref,
                  yc_ref, yl_ref, *scratch, ctx_out):
    n_c, n_l = qtc_ref.shape[2], qtl_ref.shape[2]
    n_ch = n_c + n_l
    L = qtl_ref.shape[4]
    srow = lax.broadcasted_iota(jnp.int32, (L, L), 0)
    tcol = lax.broadcasted_iota(jnp.int32, (L, L), 1)
    masks = (srow <= tcol, srow >= tcol)
    orders = (list(range(n_ch)),
              list(range(n_c - 1, -1, -1)) + list(range(n_ch - 1, n_c - 1, -1)))
    sets = (scratch[:4], scratch[4:])
    cur_refs = (g_ref, kc_ref, vtc_ref, kl_ref, vtl_ref)
    next_refs = (gn_ref, kcn_ref, vtcn_ref, kln_ref, vtln_ref)

    @pl.when((pl.program_id(0) == 0) & (pl.program_id(1) == 0))
    def _():
        for _ in _mlstm_state_pass(*cur_refs, 0, sets[0], masks, orders):
            pass

    for hh in range(2):
        if hh == 0:
            ahead = _mlstm_state_pass(*cur_refs, 1, sets[1], masks, orders)
        else:
            ahead = _mlstm_state_pass(*next_refs, 0, sets[0], masks, orders)
        _mlstm_outputs(kc_ref, qtc_ref, vtc_ref, otc_ref, kl_ref, qtl_ref, vtl_ref, otl_ref,
                       yc_ref, yl_ref, hh, sets[hh], masks, ahead, ctx_out)
        for _ in ahead:
            pass


def _mlstm_outputs(kc_ref, qtc_ref, vtc_ref, otc_ref, kl_ref, qtl_ref, vtl_ref, otl_ref,
                   yc_ref, yl_ref, hh, scratch, masks, ahead, ctx_out):
    gs_ref, rt_ref, _, st_ref = scratch
    n_c, n_l = qtc_ref.shape[2], qtl_ref.shape[2]
    L = qtl_ref.shape[4]
    lanes = slice(hh * HEAD_DIM, (hh + 1) * HEAD_DIM)

    def tick():
        next(ahead, None)

    def outputs(cidx, k, qt, vt, og):
        s_t = jnp.dot(k, qt, preferred_element_type=F32)
        carried = jnp.dot(jnp.concatenate([st_ref[0, cidx], st_ref[1, cidx]], axis=0), qt,
                          preferred_element_type=F32)
        tick()
        sws, dens, rows = [], [], []
        for d in range(2):
            gsl = gs_ref[d, cidx]
            rows.append(gsl)
            arg = jnp.where(masks[d], rt_ref[d, :, cidx:cidx + 1] + gsl[0:1], NEG_BIG)
            sw = s_t * jnp.exp2(arg)
            sws.append(sw.astype(BF16))
            qn = carried[d * STATE_ROWS + HEAD_DIM:d * STATE_ROWS + HEAD_DIM + 1]
            dens.append(jnp.sum(sw, axis=0, keepdims=True) + gsl[1:2] * qn)
            tick()
        num2 = jnp.dot(vt, jnp.concatenate(sws, axis=1), preferred_element_type=F32)
        tick()
        hsum = None
        for d in range(2):
            num = (num2[:, d * L:(d + 1) * L]
                   + rows[d][1:2] * carried[d * STATE_ROWS:d * STATE_ROWS + HEAD_DIM])
            h = num * (1.0 / jnp.maximum(jnp.abs(dens[d]), rows[d][2:3]))
            hsum = h if hsum is None else hsum + h
        ms = jnp.mean(hsum * hsum, axis=0, keepdims=True)
        y = hsum * lax.rsqrt(ms + EPS) * og
        return y.T.astype(BF16)

    if ctx_out:
        for c in range(n_c):
            yc_ref[0, c * L:(c + 1) * L, lanes] = outputs(
                c, kc_ref[0, hh, c * L:(c + 1) * L, :], qtc_ref[0, hh, c], vtc_ref[0, hh, c],
                otc_ref[0, hh, c])
            tick()
    else:
        yc_ref[0, :, lanes] = jnp.zeros((yc_ref.shape[1], HEAD_DIM), yc_ref.dtype)

    for c in range(n_l):
        yl_ref[0, c * L:(c + 1) * L, lanes] = outputs(
            n_c + c, kl_ref[0, hh, c * L:(c + 1) * L, :], qtl_ref[0, hh, c], vtl_ref[0, hh, c],
            otl_ref[0, hh, c])
        tick()


def _mlstm_call(kc, qtc, vtc, otc, kl, qtl, vtl, otl, gates, *, ctx_out):
    b = kl.shape[0]
    tc, tl = kc.shape[2], kl.shape[2]
    n_c, n_l, L = qtc.shape[2], qtl.shape[2], qtl.shape[4]
    n_ch = n_c + n_l
    assert L == LANES and n_ch <= L

    pairs = N_HEADS // 2

    def nat(t):
        return pl.BlockSpec((1, 2, t, HEAD_DIM), lambda i, j: (i, j, 0, 0))

    def tr(n):
        return pl.BlockSpec((1, 2, n, HEAD_DIM, L), lambda i, j: (i, j, 0, 0, 0))

    def nxt(*tail):
        def index(i, j):
            f = jnp.minimum(i * pairs + j + 1, b * pairs - 1)
            return (f // pairs, 2 * (f % pairs)) + tail
        return index

    def nat_next(t):
        return pl.BlockSpec((1, 1, t, HEAD_DIM), nxt(0, 0))

    def tr_next(n):
        return pl.BlockSpec((1, 1, n, HEAD_DIM, L), nxt(0, 0, 0))

    scratch_set = [pltpu.VMEM((2, n_ch, 8, L), F32),
                   pltpu.VMEM((2, L, LANES), F32),
                   pltpu.VMEM((2, n_ch, STATE_ROWS, HEAD_DIM), F32),
                   pltpu.VMEM((2, n_ch, STATE_ROWS, HEAD_DIM), BF16)]
    return pl.pallas_call(
        functools.partial(_mlstm_kernel, ctx_out=ctx_out),
        grid=(b, pairs),
        in_specs=[nat(tc), tr(n_c), tr(n_c), tr(n_c), nat(tl), tr(n_l), tr(n_l), tr(n_l),
                  pl.BlockSpec((1, 2, 4, n_ch, L), lambda i, j: (i, j, 0, 0, 0)),
                  nat_next(tc), tr_next(n_c), nat_next(tl), tr_next(n_l),
                  pl.BlockSpec((1, 1, 4, n_ch, L), nxt(0, 0, 0))],
        out_specs=[pl.BlockSpec((1, tc, 2 * HEAD_DIM), lambda i, j: (i, 0, j)),
                   pl.BlockSpec((1, tl, 2 * HEAD_DIM), lambda i, j: (i, 0, j))],
        out_shape=[jax.ShapeDtypeStruct((b, tc, D_MLSTM), BF16),
                   jax.ShapeDtypeStruct((b, tl, D_MLSTM), BF16)],
        scratch_shapes=scratch_set + scratch_set,
        compiler_params=_cparams(2),
        name="mlstm",
    )(kc, qtc, vtc, otc, kl, qtl, vtl, otl, gates, kc, vtc, kl, vtl, gates)


SEG_LPAD = 16
ROW_TILE = 64


def _conv_seg_block(z_ref, w_ref, o_ref, zp_ref, cols, seg):
    t = z_ref.shape[1]
    pitch = seg + 2 * SEG_LPAD
    n_seg = t // seg
    zp_ref[0:n_seg * pitch, :] = jnp.zeros((n_seg * pitch, LANES), F32)
    for s in range(n_seg):
        zp_ref[s * pitch + SEG_LPAD:s * pitch + SEG_LPAD + seg, :] = z_ref[0, s * seg:(s + 1) * seg, cols]
    tiles = seg // ROW_TILE
    wrows = [w_ref[j:j + 1, cols] for j in range(CONV_WIDTH)]

    def tile(i, carry):
        s, q = i // tiles, i % tiles
        src = s * pitch + q * ROW_TILE + (SEG_LPAD - CONV_PAD)
        acc = wrows[0] * zp_ref[pl.ds(src, ROW_TILE), :]
        for j in range(1, CONV_WIDTH):
            acc = acc + wrows[j] * zp_ref[pl.ds(src + j, ROW_TILE), :]
        o_ref[0, pl.ds(pl.multiple_of(i * ROW_TILE, ROW_TILE), ROW_TILE), cols] = acc
        return carry

    lax.fori_loop(0, t // ROW_TILE, tile, 0)


def _conv_rows_block(z_ref, w_ref, o_ref, zp_ref, cols):
    t = z_ref.shape[1]
    pad = CONV_PAD * ROW_TILE
    zp_ref[0:pad, :] = jnp.zeros((pad, LANES), F32)
    zp_ref[pad + t:pad + t + pad, :] = jnp.zeros((pad, LANES), F32)
    zp_ref[pad:pad + t, :] = z_ref[0, :, cols]
    wrows = [w_ref[j:j + 1, cols] for j in range(CONV_WIDTH)]

    def tile(i, carry):
        base = pl.multiple_of(i * ROW_TILE, ROW_TILE)
        acc = wrows[0] * zp_ref[pl.ds(base, ROW_TILE), :]
        for j in range(1, CONV_WIDTH):
            acc = acc + wrows[j] * zp_ref[pl.ds(base + j * ROW_TILE, ROW_TILE), :]
        o_ref[0, pl.ds(base, ROW_TILE), cols] = acc
        return carry

    lax.fori_loop(0, t // ROW_TILE, tile, 0)


def _conv_kernel(z_ref, w_ref, o_ref, zp_ref, *, plan):
    for blk, seg in enumerate(plan):
        cols = slice(blk * LANES, (blk + 1) * LANES)
        if seg is None:
            _conv_rows_block(z_ref, w_ref, o_ref, zp_ref, cols)
        else:
            _conv_seg_block(z_ref, w_ref, o_ref, zp_ref, cols, seg)


def _conv_call(z, w, plan):
    b, t, ch = z.shape
    assert ROW_TILE == GRID_W and len(plan) * LANES == ch
    rows = max((t + 2 * CONV_PAD * ROW_TILE) if seg is None else (t // seg) * (seg + 2 * SEG_LPAD)
               for seg in plan)
    return pl.pallas_call(
        functools.partial(_conv_kernel, plan=plan),
        grid=(b,),
        in_specs=[pl.BlockSpec((1, t, ch), lambda i: (i, 0, 0)),
                  pl.BlockSpec((CONV_WIDTH, ch), lambda i: (0, 0))],
        out_specs=pl.BlockSpec((1, t, ch), lambda i: (i, 0, 0)),
        out_shape=jax.ShapeDtypeStruct((b, t, ch), F32),
        scratch_shapes=[pltpu.VMEM((rows, LANES), F32)],
        compiler_params=_cparams(1),
        name="conv",
    )(z, w)


def _gate_rows(g, chunk):
    b, _, t = g.shape
    return g.reshape(b, 4, N_HEADS, t // chunk, chunk).transpose(0, 2, 1, 3, 4)


def kernel(x, c, ctx, c_ctx, w_mod, b_mod, norm_ffn1, ffn1_up, ffn1_down, norm_mix, w_in, b_in,
           mlstm_norm, conv_w, conv_b, conv_norm_g, conv_norm_b, w_out, norm_ffn2, ffn2_up,
           ffn2_down, norm_final):
    b, s, d = x.shape
    depth = w_mod.shape[0]
    d_ff = ffn1_down.shape[1]
    nck = d_ff // FF_CHUNK
    tm_c = ctx.shape[1]
    L = MLSTM_CHUNK

    def flat(a):
        return a.reshape(1, -1, a.shape[-1])

    rows = -(-(b + 1) // 8) * 8
    c_all = jnp.concatenate([c, c_ctx[None, :], jnp.zeros((rows - b - 1, d), F32)], axis=0)
    mods_all = _mod_call(c_all, w_mod, b_mod).reshape(depth, rows, N_MOD, 1, d)

    q0, k0, v0, o0 = 0, D_MLSTM, 2 * D_MLSTM, 3 * D_MLSTM
    g0 = 4 * D_MLSTM
    u0 = g0 + 4 * N_HEADS

    xc = ctx
    for l in range(depth):
        last = l == depth - 1
        mods = mods_all[l]

        def ffn_weights(up, down):
            return up.astype(BF16), down.reshape(nck, FF_CHUNK, d).astype(BF16)

        w1 = ffn_weights(ffn1_up[l], ffn1_down[l])
        x = _ffn_call(x, mods, norm_ffn1[l], *w1, js=0, mod_row=None, tm=FFN_TILE)
        xc = _ffn_call(flat(xc), mods, norm_ffn1[l], *w1, js=0, mod_row=b, tm=FFN_TILE).reshape(xc.shape)

        wi, bi = w_in[l], b_in[l]
        wn = jnp.concatenate([wi[:, k0:v0], wi[:, u0:]], axis=1).astype(BF16)
        bn = jnp.concatenate([bi[k0:v0], bi[u0:]]).reshape(1, -1)
        wt = jnp.concatenate([wi[:, q0:k0], wi[:, v0:u0]], axis=1).T.astype(BF16)
        bt = jnp.broadcast_to(jnp.concatenate([bi[q0:k0], bi[v0:u0]])[:, None],
                              (wt.shape[0], LANES))
        gain = jnp.broadcast_to(mlstm_norm[l][:, None], (D_MLSTM, LANES))
        kl, zl, gl, qtl, vtl, otl = _proj_call(x, mods, norm_mix[l], wn, bn, wt, bt, gain,
                                               mod_row=None, tm=PROJ_TILE, chunk=L)
        kc, zc, gc, qtc, vtc, otc = _proj_call(xc, mods, norm_mix[l], wn, bn, wt, bt, gain,
                                               mod_row=b, tm=tm_c, chunk=L)

        gates = jnp.concatenate([_gate_rows(gc, L), _gate_rows(gl, L)], axis=3)
        ymc, yml = _mlstm_call(kc, qtc, vtc, otc, kl, qtl, vtl, otl, gates, ctx_out=not last)

        nblk = D_CONV // LANES
        z_lat = _conv_call(zl, conv_w[l], (GRID_W,) * (nblk // 2) + (None,) * (nblk - nblk // 2))
        wm = w_out[l][:D_MLSTM].astype(BF16)
        wc = w_out[l][D_MLSTM:].astype(BF16)
        mix = (conv_b[l], conv_norm_g[l], conv_norm_b[l], wm, wc)

        w2 = ffn_weights(ffn2_up[l], ffn2_down[l])
        x = _ffn_call(x, mods, norm_ffn2[l], *w2, js=6, mod_row=None, tm=FFN_TILE,
                      final_g=norm_final if last else None, mix=(yml, [z_lat], *mix))
        if not last:
            z_ctx = _conv_call(zc, conv_w[l], (zc.shape[1],) * nblk)
            xc = _ffn_call(flat(xc), mods, norm_ffn2[l], *w2, js=6, mod_row=b, tm=FFN_TILE,
                           mix=(flat(ymc), [flat(z_ctx)], *mix)).reshape(xc.shape)
    return x
```

```python
import functools

import jax
import jax.numpy as jnp
from jax import lax
from jax.experimental import pallas as pl
from jax.experimental.pallas import tpu as pltpu

F32 = jnp.float32
BF16 = jnp.bfloat16

GRID_W = 64
N_HEADS = 4
HEAD_DIM = 128
D_MLSTM = N_HEADS * HEAD_DIM
D_CONV = 512
CONV_WIDTH = 31
CONV_PAD = (CONV_WIDTH - 1) // 2
N_MOD = 9
EPS = 1e-6

LANES = 128
FF_CHUNK = 256
FFN_TILE = 1024
PROJ_TILE = 1024
MLSTM_CHUNK = 128
FFN_UNROLL = 5
STATE_ROWS = HEAD_DIM + 16
CONV_STAGES_PER_CHUNK = 3
NEG_BIG = -1e30
LOG2E = 1.4426950408889634
VMEM_LIMIT = 56 * 1024 * 1024


def _cparams(n_axes):
    return pltpu.CompilerParams(
        dimension_semantics=("arbitrary",) * n_axes, vmem_limit_bytes=VMEM_LIMIT)


def _resident(shape):
    nd = len(shape)
    return pl.BlockSpec(shape, lambda *_: (0,) * nd, pipeline_mode=pl.Buffered(1))


def _rms_mod(x, g, shift, scale):
    ms = jnp.mean(x * x, axis=-1, keepdims=True)
    return (x * lax.rsqrt(ms + EPS)) * (g * (1.0 + scale)) + shift


def _mod_kernel(c_ref, w_ref, b_ref, o_ref):
    c = c_ref[...]
    s = (c * jax.nn.sigmoid(c)).astype(BF16)
    o_ref[0] = jnp.dot(s, w_ref[0].astype(BF16), preferred_element_type=F32) + b_ref[0]


def _mod_call(c_all, w_mod, b_mod):
    depth, d, n = w_mod.shape
    r = c_all.shape[0]
    tn = n // 8
    return pl.pallas_call(
        _mod_kernel,
        grid=(depth, n // tn),
        in_specs=[pl.BlockSpec((r, d), lambda l, j: (0, 0)),
                  pl.BlockSpec((1, d, tn), lambda l, j: (l, 0, j)),
                  pl.BlockSpec((1, 1, tn), lambda l, j: (l, 0, j))],
        out_specs=pl.BlockSpec((1, r, tn), lambda l, j: (l, 0, j)),
        out_shape=jax.ShapeDtypeStruct((depth, r, n), F32),
        compiler_params=_cparams(2),
        name="mod",
    )(c_all, w_mod, b_mod.reshape(depth, 1, n))


def _ffn_kernel(*refs, js, final, nz):
    x_ref, mod_ref = refs[:2]
    refs = refs[2:]
    if nz:
        ym_ref, z_refs = refs[0], refs[1:1 + nz]
        cb_ref, cg_ref, cnb_ref, wm_ref, wc_ref = refs[1 + nz:6 + nz]
        refs = refs[6 + nz:]
    ng_ref, wup_ref, wd_ref = refs[:3]
    refs = refs[3:]
    if final:
        gf_ref, refs = refs[0], refs[1:]
    o_ref, h_ref, a_ref, acc_ref = refs
    nck, fc, _ = wd_ref.shape
    d_ff = nck * fc

    x = x_ref[0]
    if nz:
        z = jnp.concatenate([r[0] for r in z_refs], axis=-1) if nz > 1 else z_refs[0][0]
        z = z + cb_ref[...]
        zc = z - jnp.mean(z, axis=-1, keepdims=True)
        var = jnp.mean(zc * zc, axis=-1, keepdims=True)
        ln = zc * lax.rsqrt(var + EPS) * cg_ref[...] + cnb_ref[...]
        yc = (ln * jax.nn.sigmoid(ln)).astype(BF16)
        y = (jnp.dot(ym_ref[0], wm_ref[...], preferred_element_type=F32)
             + jnp.dot(yc, wc_ref[...], preferred_element_type=F32))
        x = x + mod_ref[0, 5] * y
    o_ref[0] = x
    h_ref[...] = _rms_mod(x, ng_ref[...], mod_ref[0, js], mod_ref[0, js + 1]).astype(BF16)

    def up(c):
        h = h_ref[...]
        col = c * fc if isinstance(c, int) else pl.multiple_of(c * fc, fc)
        g = jnp.dot(h, wup_ref[:, pl.ds(col, fc)], preferred_element_type=F32)
        u = jnp.dot(h, wup_ref[:, pl.ds(d_ff + col, fc)], preferred_element_type=F32)
        return (g * jax.nn.sigmoid(g) * u).astype(BF16)

    a_ref[0] = up(0)
    acc_ref[...] = jnp.zeros_like(acc_ref)

    def chunk(c, carry):
        a_prev = a_ref[(c - 1) % 2]
        a_ref[c % 2] = up(c)
        acc_ref[...] += jnp.dot(a_prev, wd_ref[c - 1], preferred_element_type=F32)
        return carry

    lax.fori_loop(1, nck, chunk, 0, unroll=FFN_UNROLL)
    d_last = jnp.dot(a_ref[(nck - 1) % 2], wd_ref[nck - 1], preferred_element_type=F32)
    out = o_ref[0] + (0.5 * mod_ref[0, js + 2]) * (acc_ref[...] + d_last)
    if final:
        ms = jnp.mean(out * out, axis=-1, keepdims=True)
        out = out * lax.rsqrt(ms + EPS) * gf_ref[...]
    o_ref[0] = out


def _ffn_call(x, mods, ng, wup, wd, *, js, mod_row, tm, final_g=None, mix=None):
    b, t, d = x.shape
    fc = wd.shape[1]
    row = (lambda i, j: (i, 0, 0, 0)) if mod_row is None else (lambda i, j: (mod_row, 0, 0, 0))

    def tok(c):
        return pl.BlockSpec((1, tm, c), lambda i, j: (i, j, 0))

    in_specs = [tok(d), pl.BlockSpec((1, N_MOD, 1, d), row)]
    args = [x, mods]
    nz = 0
    if mix is not None:
        ym, zs, cb, cg, cnb, wm, wc = mix
        nz = len(zs)
        in_specs += [tok(ym.shape[2])] + [tok(z.shape[2]) for z in zs]
        in_specs += [_resident((1, D_CONV))] * 3 + [_resident(wm.shape), _resident(wc.shape)]
        args += [ym, *zs, cb.reshape(1, -1), cg.reshape(1, -1), cnb.reshape(1, -1), wm, wc]
    in_specs += [_resident((1, d)), _resident(wup.shape), _resident(wd.shape)]
    args += [ng.reshape(1, d), wup, wd]
    if final_g is not None:
        in_specs.append(_resident((1, d)))
        args.append(final_g.reshape(1, d))
    return pl.pallas_call(
        functools.partial(_ffn_kernel, js=js, final=final_g is not None, nz=nz),
        grid=(b, t // tm),
        in_specs=in_specs,
        out_specs=tok(d),
        out_shape=jax.ShapeDtypeStruct((b, t, d), F32),
        scratch_shapes=[pltpu.VMEM((tm, d), BF16), pltpu.VMEM((2, tm, fc), BF16),
                        pltpu.VMEM((tm, d), F32)],
        compiler_params=_cparams(2),
        name="ffn",
    )(*args)


def _proj_kernel(x_ref, mod_ref, ng_ref, wn_ref, bn_ref, wt_ref, bt_ref, gain_ref,
                 k_ref, z_ref, g_ref, qt_ref, vt_ref, ot_ref, *, chunk):
    tm = x_ref.shape[1]
    u = _rms_mod(x_ref[0], ng_ref[...], mod_ref[0, 3], mod_ref[0, 4]).astype(BF16)
    pn = jnp.dot(u, wn_ref[...], preferred_element_type=F32) + bn_ref[...]
    for h in range(N_HEADS):
        k_ref[0, h] = pn[:, h * HEAD_DIM:(h + 1) * HEAD_DIM].astype(BF16)
    a = pn[:, D_MLSTM:D_MLSTM + D_CONV]
    gl = pn[:, D_MLSTM + D_CONV:D_MLSTM + 2 * D_CONV]
    z_ref[0] = a * jax.nn.sigmoid(gl)
    pt = lax.dot_general(wt_ref[...], u, (((1,), (1,)), ((), ())), preferred_element_type=F32)
    rg = slice(3 * D_MLSTM, 3 * D_MLSTM + 4 * N_HEADS)
    for lb in range(tm // LANES):
        cols = slice(lb * LANES, (lb + 1) * LANES)
        g_ref[0, :, cols] = pt[rg, cols] + bt_ref[rg, :]
    scale = HEAD_DIM ** -0.5
    for h in range(N_HEADS):
        for lb in range(tm // LANES):
            c, off = divmod(lb * LANES, chunk)
            cols = slice(lb * LANES, (lb + 1) * LANES)
            dst = (0, h, c, slice(None), slice(off, off + LANES))
            rq = slice(h * HEAD_DIM, (h + 1) * HEAD_DIM)
            rv = slice(D_MLSTM + h * HEAD_DIM, D_MLSTM + (h + 1) * HEAD_DIM)
            ro = slice(2 * D_MLSTM + h * HEAD_DIM, 2 * D_MLSTM + (h + 1) * HEAD_DIM)
            qt_ref[dst] = ((pt[rq, cols] + bt_ref[rq, :]) * scale).astype(BF16)
            vt_ref[dst] = (pt[rv, cols] + bt_ref[rv, :]).astype(BF16)
            ot_ref[dst] = jax.nn.sigmoid(pt[ro, cols] + bt_ref[ro, :]) * gain_ref[rq, :]


def _proj_call(x, mods, ng, wn, bn, wt, bt, gain, *, mod_row, tm, chunk):
    b, t, d = x.shape
    nch, cpt = t // chunk, tm // chunk
    row = (lambda i, j: (i, 0, 0, 0)) if mod_row is None else (lambda i, j: (mod_row, 0, 0, 0))
    tspec = pl.BlockSpec((1, N_HEADS, cpt, HEAD_DIM, chunk), lambda i, j: (i, 0, j, 0, 0))
    tshape = (b, N_HEADS, nch, HEAD_DIM, chunk)
    return pl.pallas_call(
        functools.partial(_proj_kernel, chunk=chunk),
        grid=(b, t // tm),
        in_specs=[pl.BlockSpec((1, tm, d), lambda i, j: (i, j, 0)),
                  pl.BlockSpec((1, N_MOD, 1, d), row),
                  _resident((1, d)), _resident(wn.shape), _resident(bn.shape),
                  _resident(wt.shape), _resident(bt.shape), _resident(gain.shape)],
        out_specs=[pl.BlockSpec((1, N_HEADS, tm, HEAD_DIM), lambda i, j: (i, 0, j, 0)),
                   pl.BlockSpec((1, tm, D_CONV), lambda i, j: (i, j, 0)),
                   pl.BlockSpec((1, 4 * N_HEADS, tm), lambda i, j: (i, 0, j)),
                   tspec, tspec, tspec],
        out_shape=[jax.ShapeDtypeStruct((b, N_HEADS, t, HEAD_DIM), BF16),
                   jax.ShapeDtypeStruct((b, t, D_CONV), F32),
                   jax.ShapeDtypeStruct((b, 4 * N_HEADS, t), F32),
                   jax.ShapeDtypeStruct(tshape, BF16),
                   jax.ShapeDtypeStruct(tshape, BF16),
                   jax.ShapeDtypeStruct(tshape, F32)],
        compiler_params=_cparams(2),
        name="proj",
    )(x, mods, ng.reshape(1, d), wn, bn, wt, bt, gain)


def _mlstm_state_pass(g_ref, kc_ref, vtc_ref, kl_ref, vtl_ref, hh, scratch, masks, orders):
    gs_ref, rt_ref, upd_ref, st_ref = scratch
    _, n_ch, L = g_ref.shape[2:]
    for d in range(2):
        li, fpre = g_ref[0, hh, 2 * d], g_ref[0, hh, 2 * d + 1]
        lf = jnp.minimum(fpre, 0.0) - jnp.log1p(jnp.exp(-jnp.abs(fpre)))
        tri = masks[d].astype(F32)
        cum = jnp.dot(lf, tri, preferred_element_type=F32, precision=lax.Precision.HIGHEST)
        edge = L - 1 if d == 0 else 0
        total = cum[:, edge:edge + 1]
        r = li - cum
        yield
        pm = r
        lane = lax.broadcasted_iota(jnp.int32, r.shape, 1)
        s = 1
        while s < L:
            if d == 1:
                sh = jnp.where(lane < L - s, pltpu.roll(pm, L - s, 1), NEG_BIG)
            else:
                sh = jnp.where(lane >= s, pltpu.roll(pm, s, 1), NEG_BIG)
            pm = jnp.maximum(pm, sh)
            s *= 2
            yield
        rmax = pm[:, edge:edge + 1]
        m = jnp.zeros((1, 1), F32)
        m_prev, m_new = [None] * n_ch, [None] * n_ch
        for c in orders[d]:
            m_prev[c] = m
            m = total[c:c + 1] + jnp.maximum(m, rmax[c:c + 1])
            m_new[c] = m
        yield
        m_prev = jnp.concatenate(m_prev, axis=0)
        m_new = jnp.concatenate(m_new, axis=0)
        mx = jnp.maximum(m_prev, pm)
        alpha = -LOG2E * mx
        inter = jnp.exp(m_prev - mx)
        clampv = jnp.exp(-(cum + mx))
        wk = jnp.exp(total + r - m_new)
        decay = jnp.broadcast_to(jnp.exp(total + m_prev - m_new), (n_ch, L))
        rt_ref[d] = jnp.concatenate([LOG2E * r, jnp.zeros((LANES - n_ch, L), F32)], axis=0).T
        pad = jnp.zeros((3, L), F32)
        for c in range(n_ch):
            gs_ref[d, c] = jnp.concatenate(
                [alpha[c:c + 1], inter[c:c + 1], clampv[c:c + 1], wk[c:c + 1], decay[c:c + 1], pad],
                axis=0)
        yield

    n_c = kc_ref.shape[2] // L
    for c in range(n_ch):
        if c < n_c:
            k, vt = kc_ref[0, hh, c * L:(c + 1) * L, :], vtc_ref[0, hh, c]
        else:
            k, vt = kl_ref[0, hh, (c - n_c) * L:(c - n_c + 1) * L, :], vtl_ref[0, hh, c - n_c]
        vtf = vt.astype(F32)
        parts = []
        for d in range(2):
            wk = gs_ref[d, c][3:4]
            parts += [vtf * wk, wk, jnp.zeros((STATE_ROWS - HEAD_DIM - 1, L), F32)]
        lhs = jnp.concatenate(parts, axis=0).astype(BF16)
        upd = jnp.dot(lhs, k, preferred_element_type=F32)
        upd_ref[0, c] = upd[:STATE_ROWS]
        upd_ref[1, c] = upd[STATE_ROWS:]
        yield

    states = [jnp.zeros((STATE_ROWS, HEAD_DIM), F32)] * 2
    for i in range(n_ch):
        for d in range(2):
            c = orders[d][i]
            st_ref[d, c] = states[d].astype(BF16)
            states[d] = gs_ref[d, c][4:5, :HEAD_DIM] * states[d] + upd_ref[d, c]
        yield


def _conv_cols_pass(z_ref, w_ref, o_ref, zp_ref):
    seg, pitch = GRID_W, GRID_W + 2 * SEG_LPAD
    n_seg = z_ref.shape[1] // seg
    zp_ref[...] = jnp.zeros_like(zp_ref)
    yield
    for s in range(n_seg):
        zp_ref[s * pitch + SEG_LPAD:s * pitch + SEG_LPAD + seg, :] = z_ref[0, s * seg:(s + 1) * seg, :]
        if s % 8 == 7:
            yield
    for s in range(n_seg):
        src = s * pitch + SEG_LPAD - CONV_PAD
        acc = w_ref[0:1, :] * zp_ref[src:src + seg, :]
        for j in range(1, CONV_WIDTH):
            acc = acc + w_ref[j:j + 1, :] * zp_ref[src + j:src + j + seg, :]
        o_ref[0, s * seg:(s + 1) * seg, :] = acc
        yield


def _conv_rows_pass(z_ref, w_ref, o_ref):
    n = z_ref.shape[1] // GRID_W
    for i in range(n):
        acc = None
        for j in range(CONV_WIDTH):
            src = i + j - CONV_PAD
            if 0 <= src < n:
                term = w_ref[j:j + 1, :] * z_ref[0, src * GRID_W:(src + 1) * GRID_W, :]
                acc = term if acc is None else acc + term
        o_ref[0, i * GRID_W:(i + 1) * GRID_W, :] = acc
        yield


def _mlstm_kernel(kc_ref, qtc_ref, vtc_ref, otc_ref, kl_ref, qtl_ref, vtl_ref, otl_ref, g_ref,
                  kcn_ref, vtcn_ref, kln_ref, vtln_ref, gn_ref, zcol_ref, zrow_ref, wcol_ref, wrow_ref,
                  yc_ref, yl_ref, ocol_ref, orow_ref, *scratch, ctx_out):
    n_c, n_l = qtc_ref.shape[2], qtl_ref.shape[2]
    n_ch = n_c + n_l
    L = qtl_ref.shape[4]
    srow = lax.broadcasted_iota(jnp.int32, (L, L), 0)
    tcol = lax.broadcasted_iota(jnp.int32, (L, L), 1)
    masks = (srow <= tcol, srow >= tcol)
    orders = (list(range(n_ch)),
              list(range(n_c - 1, -1, -1)) + list(range(n_ch - 1, n_c - 1, -1)))
    sets = (scratch[:4], scratch[4:8])
    zp_ref = scratch[8]
    cur_refs = (g_ref, kc_ref, vtc_ref, kl_ref, vtl_ref)
    next_refs = (gn_ref, kcn_ref, vtcn_ref, kln_ref, vtln_ref)

    @pl.when((pl.program_id(0) == 0) & (pl.program_id(1) == 0))
    def _():
        for _ in _mlstm_state_pass(*cur_refs, 0, sets[0], masks, orders):
            pass

    for hh in range(2):
        if hh == 0:
            ahead = _mlstm_state_pass(*cur_refs, 1, sets[1], masks, orders)
            conv = _conv_cols_pass(zcol_ref, wcol_ref, ocol_ref, zp_ref)
        else:
            ahead = _mlstm_state_pass(*next_refs, 0, sets[0], masks, orders)
            conv = _conv_rows_pass(zrow_ref, wrow_ref, orow_ref)
        _mlstm_outputs(kc_ref, qtc_ref, vtc_ref, otc_ref, kl_ref, qtl_ref, vtl_ref, otl_ref,
                       yc_ref, yl_ref, hh, sets[hh], masks, (ahead, conv), ctx_out)
        for gen in (ahead, conv):
            for _ in gen:
                pass


def _mlstm_outputs(kc_ref, qtc_ref, vtc_ref, otc_ref, kl_ref, qtl_ref, vtl_ref, otl_ref,
                   yc_ref, yl_ref, hh, scratch, masks, fillers, ctx_out):
    gs_ref, rt_ref, _, st_ref = scratch
    n_c, n_l = qtc_ref.shape[2], qtl_ref.shape[2]
    L = qtl_ref.shape[4]
    lanes = slice(hh * HEAD_DIM, (hh + 1) * HEAD_DIM)
    ahead, conv = fillers

    def tick(boundary=False):
        next(ahead, None)
        if boundary:
            for _ in range(CONV_STAGES_PER_CHUNK):
                next(conv, None)

    def outputs(cidx, k, qt, vt, og):
        s_t = jnp.dot(k, qt, preferred_element_type=F32)
        carried = jnp.dot(jnp.concatenate([st_ref[0, cidx], st_ref[1, cidx]], axis=0), qt,
                          preferred_element_type=F32)
        tick()
        sws, dens, rows = [], [], []
        for d in range(2):
            gsl = gs_ref[d, cidx]
            rows.append(gsl)
            arg = jnp.where(masks[d], rt_ref[d, :, cidx:cidx + 1] + gsl[0:1], NEG_BIG)
            sw = s_t * jnp.exp2(arg)
            sws.append(sw.astype(BF16))
            qn = carried[d * STATE_ROWS + HEAD_DIM:d * STATE_ROWS + HEAD_DIM + 1]
            dens.append(jnp.sum(sw, axis=0, keepdims=True) + gsl[1:2] * qn)
            tick()
        num2 = jnp.dot(vt, jnp.concatenate(sws, axis=1), preferred_element_type=F32)
        tick()
        hsum = None
        for d in range(2):
            num = (num2[:, d * L:(d + 1) * L]
                   + rows[d][1:2] * carried[d * STATE_ROWS:d * STATE_ROWS + HEAD_DIM])
            h = num * (1.0 / jnp.maximum(jnp.abs(dens[d]), rows[d][2:3]))
            hsum = h if hsum is None else hsum + h
        ms = jnp.mean(hsum * hsum, axis=0, keepdims=True)
        y = hsum * lax.rsqrt(ms + EPS) * og
        return y.T.astype(BF16)

    if ctx_out:
        for c in range(n_c):
            yc_ref[0, c * L:(c + 1) * L, lanes] = outputs(
                c, kc_ref[0, hh, c * L:(c + 1) * L, :], qtc_ref[0, hh, c], vtc_ref[0, hh, c],
                otc_ref[0, hh, c])
            tick(boundary=True)
    else:
        yc_ref[0, :, lanes] = jnp.zeros((yc_ref.shape[1], HEAD_DIM), yc_ref.dtype)

    for c in range(n_l):
        yl_ref[0, c * L:(c + 1) * L, lanes] = outputs(
            n_c + c, kl_ref[0, hh, c * L:(c + 1) * L, :], qtl_ref[0, hh, c], vtl_ref[0, hh, c],
            otl_ref[0, hh, c])
        tick(boundary=True)


def _mlstm_call(kc, qtc, vtc, otc, kl, qtl, vtl, otl, gates, z, w_conv, *, ctx_out):
    b = kl.shape[0]
    tc, tl = kc.shape[2], kl.shape[2]
    n_c, n_l, L = qtc.shape[2], qtl.shape[2], qtl.shape[4]
    n_ch = n_c + n_l
    assert L == LANES and n_ch <= L

    pairs = N_HEADS // 2

    def nat(t):
        return pl.BlockSpec((1, 2, t, HEAD_DIM), lambda i, j: (i, j, 0, 0))

    def tr(n):
        return pl.BlockSpec((1, 2, n, HEAD_DIM, L), lambda i, j: (i, j, 0, 0, 0))

    def nxt(*tail):
        def index(i, j):
            f = jnp.minimum(i * pairs + j + 1, b * pairs - 1)
            return (f // pairs, 2 * (f % pairs)) + tail
        return index

    def nat_next(t):
        return pl.BlockSpec((1, 1, t, HEAD_DIM), nxt(0, 0))

    def tr_next(n):
        return pl.BlockSpec((1, 1, n, HEAD_DIM, L), nxt(0, 0, 0))

    scratch_set = [pltpu.VMEM((2, n_ch, 8, L), F32),
                   pltpu.VMEM((2, L, LANES), F32),
                   pltpu.VMEM((2, n_ch, STATE_ROWS, HEAD_DIM), F32),
                   pltpu.VMEM((2, n_ch, STATE_ROWS, HEAD_DIM), BF16)]
    assert D_CONV == 2 * pairs * LANES and tl % GRID_W == 0
    zblk = (1, tl, LANES)
    half = D_CONV // 2
    return pl.pallas_call(
        functools.partial(_mlstm_kernel, ctx_out=ctx_out),
        grid=(b, pairs),
        in_specs=[nat(tc), tr(n_c), tr(n_c), tr(n_c), nat(tl), tr(n_l), tr(n_l), tr(n_l),
                  pl.BlockSpec((1, 2, 4, n_ch, L), lambda i, j: (i, j, 0, 0, 0)),
                  nat_next(tc), tr_next(n_c), nat_next(tl), tr_next(n_l),
                  pl.BlockSpec((1, 1, 4, n_ch, L), nxt(0, 0, 0)),
                  pl.BlockSpec(zblk, lambda i, j: (i, 0, j)),
                  pl.BlockSpec(zblk, lambda i, j: (i, 0, pairs + j)),
                  pl.BlockSpec((CONV_WIDTH, LANES), lambda i, j: (0, j)),
                  pl.BlockSpec((CONV_WIDTH, LANES), lambda i, j: (0, pairs + j))],
        out_specs=[pl.BlockSpec((1, tc, 2 * HEAD_DIM), lambda i, j: (i, 0, j)),
                   pl.BlockSpec((1, tl, 2 * HEAD_DIM), lambda i, j: (i, 0, j)),
                   pl.BlockSpec(zblk, lambda i, j: (i, 0, j)),
                   pl.BlockSpec(zblk, lambda i, j: (i, 0, j))],
        out_shape=[jax.ShapeDtypeStruct((b, tc, D_MLSTM), BF16),
                   jax.ShapeDtypeStruct((b, tl, D_MLSTM), BF16),
                   jax.ShapeDtypeStruct((b, tl, half), F32),
                   jax.ShapeDtypeStruct((b, tl, half), F32)],
        scratch_shapes=scratch_set + scratch_set
                       + [pltpu.VMEM(((tl // GRID_W) * (GRID_W + 2 * SEG_LPAD), LANES), F32)],
        compiler_params=_cparams(2),
        name="mlstm",
    )(kc, qtc, vtc, otc, kl, qtl, vtl, otl, gates, kc, vtc, kl, vtl, gates, z, z, w_conv, w_conv)


SEG_LPAD = 16
ROW_TILE = 64


def _conv_seg_block(z_ref, w_ref, o_ref, zp_ref, cols, seg):
    t = z_ref.shape[1]
    pitch = seg + 2 * SEG_LPAD
    n_seg = t // seg
    zp_ref[0:n_seg * pitch, :] = jnp.zeros((n_seg * pitch, LANES), F32)
    for s in range(n_seg):
        zp_ref[s * pitch + SEG_LPAD:s * pitch + SEG_LPAD + seg, :] = z_ref[0, s * seg:(s + 1) * seg, cols]
    tiles = seg // ROW_TILE
    wrows = [w_ref[j:j + 1, cols] for j in range(CONV_WIDTH)]

    def tile(i, carry):
        s, q = i // tiles, i % tiles
        src = s * pitch + q * ROW_TILE + (SEG_LPAD - CONV_PAD)
        acc = wrows[0] * zp_ref[pl.ds(src, ROW_TILE), :]
        for j in range(1, CONV_WIDTH):
            acc = acc + wrows[j] * zp_ref[pl.ds(src + j, ROW_TILE), :]
        o_ref[0, pl.ds(pl.multiple_of(i * ROW_TILE, ROW_TILE), ROW_TILE), cols] = acc
        return carry

    lax.fori_loop(0, t // ROW_TILE, tile, 0)


def _conv_rows_block(z_ref, w_ref, o_ref, zp_ref, cols):
    t = z_ref.shape[1]
    pad = CONV_PAD * ROW_TILE
    zp_ref[0:pad, :] = jnp.zeros((pad, LANES), F32)
    zp_ref[pad + t:pad + t + pad, :] = jnp.zeros((pad, LANES), F32)
    zp_ref[pad:pad + t, :] = z_ref[0, :, cols]
    wrows = [w_ref[j:j + 1, cols] for j in range(CONV_WIDTH)]

    def tile(i, carry):
        base = pl.multiple_of(i * ROW_TILE, ROW_TILE)
        acc = wrows[0] * zp_ref[pl.ds(base, ROW_TILE), :]
        for j in range(1, CONV_WIDTH):
            acc = acc + wrows[j] * zp_ref[pl.ds(base + j * ROW_TILE, ROW_TILE), :]
        o_ref[0, pl.ds(base, ROW_TILE), cols] = acc
        return carry

    lax.fori_loop(0, t // ROW_TILE, tile, 0)


def _conv_kernel(z_ref, w_ref, o_ref, zp_ref, *, plan):
    for blk, seg in enumerate(plan):
        cols = slice(blk * LANES, (blk + 1) * LANES)
        if seg is None:
            _conv_rows_block(z_ref, w_ref, o_ref, zp_ref, cols)
        else:
            _conv_seg_block(z_ref, w_ref, o_ref, zp_ref, cols, seg)


def _conv_call(z, w, plan):
    b, t, ch = z.shape
    assert ROW_TILE == GRID_W and len(plan) * LANES == ch
    rows = max((t + 2 * CONV_PAD * ROW_TILE) if seg is None else (t // seg) * (seg + 2 * SEG_LPAD)
               for seg in plan)
    return pl.pallas_call(
        functools.partial(_conv_kernel, plan=plan),
        grid=(b,),
        in_specs=[pl.BlockSpec((1, t, ch), lambda i: (i, 0, 0)),
                  pl.BlockSpec((CONV_WIDTH, ch), lambda i: (0, 0))],
        out_specs=pl.BlockSpec((1, t, ch), lambda i: (i, 0, 0)),
        out_shape=jax.ShapeDtypeStruct((b, t, ch), F32),
        scratch_shapes=[pltpu.VMEM((rows, LANES), F32)],
        compiler_params=_cparams(1),
        name="conv",
    )(z, w)


def _gate_rows(g, chunk):
    b, _, t = g.shape
    return g.reshape(b, 4, N_HEADS, t // chunk, chunk).transpose(0, 2, 1, 3, 4)


def kernel(x, c, ctx, c_ctx, w_mod, b_mod, norm_ffn1, ffn1_up, ffn1_down, norm_mix, w_in, b_in,
           mlstm_norm, conv_w, conv_b, conv_norm_g, conv_norm_b, w_out, norm_ffn2, ffn2_up,
           ffn2_down, norm_final):
    b, s, d = x.shape
    depth = w_mod.shape[0]
    d_ff = ffn1_down.shape[1]
    nck = d_ff // FF_CHUNK
    tm_c = ctx.shape[1]
    L = MLSTM_CHUNK

    def flat(a):
        return a.reshape(1, -1, a.shape[-1])

    rows = -(-(b + 1) // 8) * 8
    c_all = jnp.concatenate([c, c_ctx[None, :], jnp.zeros((rows - b - 1, d), F32)], axis=0)
    mods_all = _mod_call(c_all, w_mod, b_mod).reshape(depth, rows, N_MOD, 1, d)

    q0, k0, v0, o0 = 0, D_MLSTM, 2 * D_MLSTM, 3 * D_MLSTM
    g0 = 4 * D_MLSTM
    u0 = g0 + 4 * N_HEADS

    xc = ctx
    for l in range(depth):
        last = l == depth - 1
        mods = mods_all[l]

        def ffn_weights(up, down):
            return up.astype(BF16), down.reshape(nck, FF_CHUNK, d).astype(BF16)

        w1 = ffn_weights(ffn1_up[l], ffn1_down[l])
        x = _ffn_call(x, mods, norm_ffn1[l], *w1, js=0, mod_row=None, tm=FFN_TILE)
        xc = _ffn_call(flat(xc), mods, norm_ffn1[l], *w1, js=0, mod_row=b, tm=FFN_TILE).reshape(xc.shape)

        wi, bi = w_in[l], b_in[l]
        wn = jnp.concatenate([wi[:, k0:v0], wi[:, u0:]], axis=1).astype(BF16)
        bn = jnp.concatenate([bi[k0:v0], bi[u0:]]).reshape(1, -1)
        wt = jnp.concatenate([wi[:, q0:k0], wi[:, v0:u0]], axis=1).T.astype(BF16)
        bt = jnp.broadcast_to(jnp.concatenate([bi[q0:k0], bi[v0:u0]])[:, None],
                              (wt.shape[0], LANES))
        gain = jnp.broadcast_to(mlstm_norm[l][:, None], (D_MLSTM, LANES))
        kl, zl, gl, qtl, vtl, otl = _proj_call(x, mods, norm_mix[l], wn, bn, wt, bt, gain,
                                               mod_row=None, tm=PROJ_TILE, chunk=L)
        kc, zc, gc, qtc, vtc, otc = _proj_call(xc, mods, norm_mix[l], wn, bn, wt, bt, gain,
                                               mod_row=b, tm=tm_c, chunk=L)

        gates = jnp.concatenate([_gate_rows(gc, L), _gate_rows(gl, L)], axis=3)
        ymc, yml, z_cols, z_rows = _mlstm_call(kc, qtc, vtc, otc, kl, qtl, vtl, otl, gates,
                                               zl, conv_w[l], ctx_out=not last)
        nblk = D_CONV // LANES
        wm = w_out[l][:D_MLSTM].astype(BF16)
        wc = w_out[l][D_MLSTM:].astype(BF16)
        mix = (conv_b[l], conv_norm_g[l], conv_norm_b[l], wm, wc)

        w2 = ffn_weights(ffn2_up[l], ffn2_down[l])
        x = _ffn_call(x, mods, norm_ffn2[l], *w2, js=6, mod_row=None, tm=FFN_TILE,
                      final_g=norm_final if last else None, mix=(yml, [z_cols, z_rows], *mix))
        if not last:
            z_ctx = _conv_call(zc, conv_w[l], (zc.shape[1],) * nblk)
            xc = _ffn_call(flat(xc), mods, norm_ffn2[l], *w2, js=6, mod_row=b, tm=FFN_TILE,
                           mix=(flat(ymc), [flat(z_ctx)], *mix)).reshape(xc.shape)
    return x
```

```python
import functools

import jax
import jax.numpy as jnp
from jax import lax
from jax.experimental import pallas as pl
from jax.experimental.pallas import tpu as pltpu

F32 = jnp.float32
BF16 = jnp.bfloat16

GRID_W = 64
N_HEADS = 4
HEAD_DIM = 128
D_MLSTM = N_HEADS * HEAD_DIM
D_CONV = 512
CONV_WIDTH = 31
CONV_PAD = (CONV_WIDTH - 1) // 2
N_MOD = 9
EPS = 1e-6

LANES = 128
FF_CHUNK = 256
FFN_TILE = 1024
PROJ_TILE = 1024
FFN_UNROLL = 10
MLSTM_CHUNK = 128
STATE_ROWS = HEAD_DIM + 16
CONV_STAGES_PER_CHUNK = 3
NEG_BIG = -1e30
LOG2E = 1.4426950408889634
VMEM_LIMIT = 56 * 1024 * 1024


def _cparams(n_axes):
    return pltpu.CompilerParams(
        dimension_semantics=("arbitrary",) * n_axes, vmem_limit_bytes=VMEM_LIMIT)


def _resident(shape):
    nd = len(shape)
    return pl.BlockSpec(shape, lambda *_: (0,) * nd, pipeline_mode=pl.Buffered(1))


def _rms_mod(x, g, shift, scale):
    ms = jnp.mean(x * x, axis=-1, keepdims=True)
    return (x * lax.rsqrt(ms + EPS)) * (g * (1.0 + scale)) + shift


def _mod_kernel(c_ref, w_ref, b_ref, o_ref):
    c = c_ref[...]
    s = (c * jax.nn.sigmoid(c)).astype(BF16)
    o_ref[0] = jnp.dot(s, w_ref[0].astype(BF16), preferred_element_type=F32) + b_ref[0]


def _mod_call(c_all, w_mod, b_mod):
    depth, d, n = w_mod.shape
    r = c_all.shape[0]
    tn = n // 8
    return pl.pallas_call(
        _mod_kernel,
        grid=(depth, n // tn),
        in_specs=[pl.BlockSpec((r, d), lambda l, j: (0, 0)),
                  pl.BlockSpec((1, d, tn), lambda l, j: (l, 0, j)),
                  pl.BlockSpec((1, 1, tn), lambda l, j: (l, 0, j))],
        out_specs=pl.BlockSpec((1, r, tn), lambda l, j: (l, 0, j)),
        out_shape=jax.ShapeDtypeStruct((depth, r, n), F32),
        compiler_params=_cparams(2),
        name="mod",
    )(c_all, w_mod, b_mod.reshape(depth, 1, n))


def _ffn_kernel(*refs, js, final, nz):
    x_ref, mod_ref = refs[:2]
    refs = refs[2:]
    if nz:
        ym_ref, z_refs = refs[0], refs[1:1 + nz]
        cb_ref, cg_ref, cnb_ref, wm_ref, wc_ref = refs[1 + nz:6 + nz]
        refs = refs[6 + nz:]
    ng_ref, wup_ref, wd_ref = refs[:3]
    refs = refs[3:]
    if final:
        gf_ref, refs = refs[0], refs[1:]
    o_ref, h_ref, a_ref, acc_ref = refs
    nck, fc, _ = wd_ref.shape
    d_ff = nck * fc

    x = x_ref[0]
    if nz:
        z = jnp.concatenate([r[0] for r in z_refs], axis=-1) if nz > 1 else z_refs[0][0]
        z = z + cb_ref[...]
        zc = z - jnp.mean(z, axis=-1, keepdims=True)
        var = jnp.mean(zc * zc, axis=-1, keepdims=True)
        ln = zc * lax.rsqrt(var + EPS) * cg_ref[...] + cnb_ref[...]
        yc = (ln * jax.nn.sigmoid(ln)).astype(BF16)
        y = (jnp.dot(ym_ref[0], wm_ref[...], preferred_element_type=F32)
             + jnp.dot(yc, wc_ref[...], preferred_element_type=F32))
        x = x + mod_ref[0, 5] * y
    o_ref[0] = x
    h_ref[...] = _rms_mod(x, ng_ref[...], mod_ref[0, js], mod_ref[0, js + 1]).astype(BF16)

    def up(c):
        h = h_ref[...]
        col = c * fc if isinstance(c, int) else pl.multiple_of(c * fc, fc)
        g = jnp.dot(h, wup_ref[:, pl.ds(col, fc)], preferred_element_type=F32)
        u = jnp.dot(h, wup_ref[:, pl.ds(d_ff + col, fc)], preferred_element_type=F32)
        return (g * jax.nn.sigmoid(g) * u).astype(BF16)

    a_ref[0] = up(0)
    acc_ref[...] = jnp.zeros_like(acc_ref)

    def chunk(c, carry):
        a_prev = a_ref[(c - 1) % 2]
        a_ref[c % 2] = up(c)
        acc_ref[...] += jnp.dot(a_prev, wd_ref[c - 1], preferred_element_type=F32)
        return carry

    lax.fori_loop(1, nck, chunk, 0, unroll=FFN_UNROLL)
    d_last = jnp.dot(a_ref[(nck - 1) % 2], wd_ref[nck - 1], preferred_element_type=F32)
    out = o_ref[0] + (0.5 * mod_ref[0, js + 2]) * (acc_ref[...] + d_last)
    if final:
        ms = jnp.mean(out * out, axis=-1, keepdims=True)
        out = out * lax.rsqrt(ms + EPS) * gf_ref[...]
    o_ref[0] = out


def _ffn_call(x, mods, ng, wup, wd, *, js, mod_row, tm, final_g=None, mix=None):
    b, t, d = x.shape
    fc = wd.shape[1]
    row = (lambda i, j: (i, 0, 0, 0)) if mod_row is None else (lambda i, j: (mod_row, 0, 0, 0))

    def tok(c):
        return pl.BlockSpec((1, tm, c), lambda i, j: (i, j, 0))

    in_specs = [tok(d), pl.BlockSpec((1, N_MOD, 1, d), row)]
    args = [x, mods]
    nz = 0
    if mix is not None:
        ym, zs, cb, cg, cnb, wm, wc = mix
        nz = len(zs)
        in_specs += [tok(ym.shape[2])] + [tok(z.shape[2]) for z in zs]
        in_specs += [_resident((1, D_CONV))] * 3 + [_resident(wm.shape), _resident(wc.shape)]
        args += [ym, *zs, cb.reshape(1, -1), cg.reshape(1, -1), cnb.reshape(1, -1), wm, wc]
    in_specs += [_resident((1, d)), _resident(wup.shape), _resident(wd.shape)]
    args += [ng.reshape(1, d), wup, wd]
    if final_g is not None:
        in_specs.append(_resident((1, d)))
        args.append(final_g.reshape(1, d))
    return pl.pallas_call(
        functools.partial(_ffn_kernel, js=js, final=final_g is not None, nz=nz),
        grid=(b, t // tm),
        in_specs=in_specs,
        out_specs=tok(d),
        out_shape=jax.ShapeDtypeStruct((b, t, d), F32),
        scratch_shapes=[pltpu.VMEM((tm, d), BF16), pltpu.VMEM((2, tm, fc), BF16),
                        pltpu.VMEM((tm, d), F32)],
        compiler_params=_cparams(2),
        name="ffn",
    )(*args)


def _proj_kernel(x_ref, mod_ref, ng_ref, wn_ref, bn_ref, wt_ref, bt_ref, gain_ref,
                 k_ref, z_ref, g_ref, qt_ref, vt_ref, ot_ref, *, chunk):
    tm = x_ref.shape[1]
    u = _rms_mod(x_ref[0], ng_ref[...], mod_ref[0, 3], mod_ref[0, 4]).astype(BF16)
    pn = jnp.dot(u, wn_ref[...], preferred_element_type=F32) + bn_ref[...]
    for h in range(N_HEADS):
        k_ref[0, h] = pn[:, h * HEAD_DIM:(h + 1) * HEAD_DIM].astype(BF16)
    a = pn[:, D_MLSTM:D_MLSTM + D_CONV]
    gl = pn[:, D_MLSTM + D_CONV:D_MLSTM + 2 * D_CONV]
    z_ref[0] = a * jax.nn.sigmoid(gl)
    pt = lax.dot_general(wt_ref[...], u, (((1,), (1,)), ((), ())), preferred_element_type=F32)
    rg = slice(3 * D_MLSTM, 3 * D_MLSTM + 4 * N_HEADS)
    for lb in range(tm // LANES):
        cols = slice(lb * LANES, (lb + 1) * LANES)
        g_ref[0, :, cols] = pt[rg, cols] + bt_ref[rg, :]
    scale = HEAD_DIM ** -0.5
    for h in range(N_HEADS):
        for lb in range(tm // LANES):
            c, off = divmod(lb * LANES, chunk)
            cols = slice(lb * LANES, (lb + 1) * LANES)
            dst = (0, h, c, slice(None), slice(off, off + LANES))
            rq = slice(h * HEAD_DIM, (h + 1) * HEAD_DIM)
            rv = slice(D_MLSTM + h * HEAD_DIM, D_MLSTM + (h + 1) * HEAD_DIM)
            ro = slice(2 * D_MLSTM + h * HEAD_DIM, 2 * D_MLSTM + (h + 1) * HEAD_DIM)
            qt_ref[dst] = ((pt[rq, cols] + bt_ref[rq, :]) * scale).astype(BF16)
            vt_ref[dst] = (pt[rv, cols] + bt_ref[rv, :]).astype(BF16)
            ot_ref[dst] = jax.nn.sigmoid(pt[ro, cols] + bt_ref[ro, :]) * gain_ref[rq, :]


def _proj_call(x, mods, ng, wn, bn, wt, bt, gain, *, mod_row, tm, chunk):
    b, t, d = x.shape
    nch, cpt = t // chunk, tm // chunk
    row = (lambda i, j: (i, 0, 0, 0)) if mod_row is None else (lambda i, j: (mod_row, 0, 0, 0))
    tspec = pl.BlockSpec((1, N_HEADS, cpt, HEAD_DIM, chunk), lambda i, j: (i, 0, j, 0, 0))
    tshape = (b, N_HEADS, nch, HEAD_DIM, chunk)
    return pl.pallas_call(
        functools.partial(_proj_kernel, chunk=chunk),
        grid=(b, t // tm),
        in_specs=[pl.BlockSpec((1, tm, d), lambda i, j: (i, j, 0)),
                  pl.BlockSpec((1, N_MOD, 1, d), row),
                  _resident((1, d)), _resident(wn.shape), _resident(bn.shape),
                  _resident(wt.shape), _resident(bt.shape), _resident(gain.shape)],
        out_specs=[pl.BlockSpec((1, N_HEADS, tm, HEAD_DIM), lambda i, j: (i, 0, j, 0)),
                   pl.BlockSpec((1, tm, D_CONV), lambda i, j: (i, j, 0)),
                   pl.BlockSpec((1, 4 * N_HEADS, tm), lambda i, j: (i, 0, j)),
                   tspec, tspec, tspec],
        out_shape=[jax.ShapeDtypeStruct((b, N_HEADS, t, HEAD_DIM), BF16),
                   jax.ShapeDtypeStruct((b, t, D_CONV), F32),
                   jax.ShapeDtypeStruct((b, 4 * N_HEADS, t), F32),
                   jax.ShapeDtypeStruct(tshape, BF16),
                   jax.ShapeDtypeStruct(tshape, BF16),
                   jax.ShapeDtypeStruct(tshape, F32)],
        compiler_params=_cparams(2),
        name="proj",
    )(x, mods, ng.reshape(1, d), wn, bn, wt, bt, gain)


def _mlstm_state_pass(g_ref, kc_ref, vtc_ref, kl_ref, vtl_ref, hh, scratch, masks, orders):
    gs_ref, rt_ref, upd_ref, st_ref = scratch
    _, n_ch, L = g_ref.shape[2:]
    for d in range(2):
        li, fpre = g_ref[0, hh, 2 * d], g_ref[0, hh, 2 * d + 1]
        lf = jnp.minimum(fpre, 0.0) - jnp.log1p(jnp.exp(-jnp.abs(fpre)))
        tri = masks[d].astype(F32)
        cum = jnp.dot(lf, tri, preferred_element_type=F32, precision=lax.Precision.HIGHEST)
        edge = L - 1 if d == 0 else 0
        total = cum[:, edge:edge + 1]
        r = li - cum
        yield
        pm = r
        lane = lax.broadcasted_iota(jnp.int32, r.shape, 1)
        s = 1
        while s < L:
            if d == 1:
                sh = jnp.where(lane < L - s, pltpu.roll(pm, L - s, 1), NEG_BIG)
            else:
                sh = jnp.where(lane >= s, pltpu.roll(pm, s, 1), NEG_BIG)
            pm = jnp.maximum(pm, sh)
            s *= 2
            yield
        rmax = pm[:, edge:edge + 1]
        m = jnp.zeros((1, 1), F32)
        m_prev, m_new = [None] * n_ch, [None] * n_ch
        for c in orders[d]:
            m_prev[c] = m
            m = total[c:c + 1] + jnp.maximum(m, rmax[c:c + 1])
            m_new[c] = m
        yield
        m_prev = jnp.concatenate(m_prev, axis=0)
        m_new = jnp.concatenate(m_new, axis=0)
        mx = jnp.maximum(m_prev, pm)
        alpha = -LOG2E * mx
        inter = jnp.exp(m_prev - mx)
        clampv = jnp.exp(-(cum + mx))
        wk = jnp.exp(total + r - m_new)
        decay = jnp.broadcast_to(jnp.exp(total + m_prev - m_new), (n_ch, L))
        rt_ref[d] = jnp.concatenate([LOG2E * r, jnp.zeros((LANES - n_ch, L), F32)], axis=0).T
        pad = jnp.zeros((3, L), F32)
        for c in range(n_ch):
            gs_ref[d, c] = jnp.concatenate(
                [alpha[c:c + 1], inter[c:c + 1], clampv[c:c + 1], wk[c:c + 1], decay[c:c + 1], pad],
                axis=0)
        yield

    n_c = kc_ref.shape[2] // L
    for c in range(n_ch):
        if c < n_c:
            k, vt = kc_ref[0, hh, c * L:(c + 1) * L, :], vtc_ref[0, hh, c]
        else:
            k, vt = kl_ref[0, hh, (c - n_c) * L:(c - n_c + 1) * L, :], vtl_ref[0, hh, c - n_c]
        vtf = vt.astype(F32)
        parts = []
        for d in range(2):
            wk = gs_ref[d, c][3:4]
            parts += [vtf * wk, wk, jnp.zeros((STATE_ROWS - HEAD_DIM - 1, L), F32)]
        lhs = jnp.concatenate(parts, axis=0).astype(BF16)
        upd = jnp.dot(lhs, k, preferred_element_type=F32)
        upd_ref[0, c] = upd[:STATE_ROWS]
        upd_ref[1, c] = upd[STATE_ROWS:]
        yield

    states = [jnp.zeros((STATE_ROWS, HEAD_DIM), F32)] * 2
    for i in range(n_ch):
        for d in range(2):
            c = orders[d][i]
            st_ref[d, c] = states[d].astype(BF16)
            states[d] = gs_ref[d, c][4:5, :HEAD_DIM] * states[d] + upd_ref[d, c]
        yield


def _conv_cols_pass(z_ref, w_ref, o_ref, zp_ref):
    seg, pitch = GRID_W, GRID_W + 2 * SEG_LPAD
    n_seg = z_ref.shape[1] // seg
    zp_ref[...] = jnp.zeros_like(zp_ref)
    yield
    for s in range(n_seg):
        zp_ref[s * pitch + SEG_LPAD:s * pitch + SEG_LPAD + seg, :] = z_ref[0, s * seg:(s + 1) * seg, :]
        if s % 8 == 7:
            yield
    for s in range(n_seg):
        src = s * pitch + SEG_LPAD - CONV_PAD
        acc = w_ref[0:1, :] * zp_ref[src:src + seg, :]
        for j in range(1, CONV_WIDTH):
            acc = acc + w_ref[j:j + 1, :] * zp_ref[src + j:src + j + seg, :]
        o_ref[0, s * seg:(s + 1) * seg, :] = acc
        yield


def _conv_rows_pass(z_ref, w_ref, o_ref):
    n = z_ref.shape[1] // GRID_W
    for i in range(n):
        acc = None
        for j in range(CONV_WIDTH):
            src = i + j - CONV_PAD
            if 0 <= src < n:
                term = w_ref[j:j + 1, :] * z_ref[0, src * GRID_W:(src + 1) * GRID_W, :]
                acc = term if acc is None else acc + term
        o_ref[0, i * GRID_W:(i + 1) * GRID_W, :] = acc
        yield


def _mlstm_kernel(kc_ref, qtc_ref, vtc_ref, otc_ref, kl_ref, qtl_ref, vtl_ref, otl_ref, g_ref,
                  kcn_ref, vtcn_ref, kln_ref, vtln_ref, gn_ref, zcol_ref, zrow_ref, wcol_ref, wrow_ref,
                  yc_ref, yl_ref, ocol_ref, orow_ref, *scratch, ctx_out):
    n_c, n_l = qtc_ref.shape[2], qtl_ref.shape[2]
    n_ch = n_c + n_l
    L = qtl_ref.shape[4]
    srow = lax.broadcasted_iota(jnp.int32, (L, L), 0)
    tcol = lax.broadcasted_iota(jnp.int32, (L, L), 1)
    masks = (srow <= tcol, srow >= tcol)
    orders = (list(range(n_ch)),
              list(range(n_c - 1, -1, -1)) + list(range(n_ch - 1, n_c - 1, -1)))
    sets = (scratch[:4], scratch[4:8])
    zp_ref = scratch[8]
    cur_refs = (g_ref, kc_ref, vtc_ref, kl_ref, vtl_ref)
    next_refs = (gn_ref, kcn_ref, vtcn_ref, kln_ref, vtln_ref)

    @pl.when((pl.program_id(0) == 0) & (pl.program_id(1) == 0))
    def _():
        for _ in _mlstm_state_pass(*cur_refs, 0, sets[0], masks, orders):
            pass

    for hh in range(2):
        if hh == 0:
            ahead = _mlstm_state_pass(*cur_refs, 1, sets[1], masks, orders)
            conv = _conv_cols_pass(zcol_ref, wcol_ref, ocol_ref, zp_ref)
        else:
            ahead = _mlstm_state_pass(*next_refs, 0, sets[0], masks, orders)
            conv = _conv_rows_pass(zrow_ref, wrow_ref, orow_ref)
        _mlstm_outputs(kc_ref, qtc_ref, vtc_ref, otc_ref, kl_ref, qtl_ref, vtl_ref, otl_ref,
                       yc_ref, yl_ref, hh, sets[hh], masks, (ahead, conv), ctx_out)
        for gen in (ahead, conv):
            for _ in gen:
                pass


def _mlstm_outputs(kc_ref, qtc_ref, vtc_ref, otc_ref, kl_ref, qtl_ref, vtl_ref, otl_ref,
                   yc_ref, yl_ref, hh, scratch, masks, fillers, ctx_out):
    gs_ref, rt_ref, _, st_ref = scratch
    n_c, n_l = qtc_ref.shape[2], qtl_ref.shape[2]
    L = qtl_ref.shape[4]
    lanes = slice(hh * HEAD_DIM, (hh + 1) * HEAD_DIM)
    ahead, conv = fillers

    def tick(boundary=False):
        next(ahead, None)
        if boundary:
            for _ in range(CONV_STAGES_PER_CHUNK):
                next(conv, None)

    def outputs(cidx, k, qt, vt, og):
        s_t = jnp.dot(k, qt, preferred_element_type=F32)
        carried = jnp.dot(jnp.concatenate([st_ref[0, cidx], st_ref[1, cidx]], axis=0), qt,
                          preferred_element_type=F32)
        tick()
        sws, dens, rows = [], [], []
        for d in range(2):
            gsl = gs_ref[d, cidx]
            rows.append(gsl)
            arg = jnp.where(masks[d], rt_ref[d, :, cidx:cidx + 1] + gsl[0:1], NEG_BIG)
            sw = s_t * jnp.exp2(arg)
            sws.append(sw.astype(BF16))
            qn = carried[d * STATE_ROWS + HEAD_DIM:d * STATE_ROWS + HEAD_DIM + 1]
            dens.append(jnp.sum(sw, axis=0, keepdims=True) + gsl[1:2] * qn)
            tick()
        num2 = jnp.dot(vt, jnp.concatenate(sws, axis=1), preferred_element_type=F32)
        tick()
        hsum = None
        for d in range(2):
            num = (num2[:, d * L:(d + 1) * L]
                   + rows[d][1:2] * carried[d * STATE_ROWS:d * STATE_ROWS + HEAD_DIM])
            h = num * (1.0 / jnp.maximum(jnp.abs(dens[d]), rows[d][2:3]))
            hsum = h if hsum is None else hsum + h
        ms = jnp.mean(hsum * hsum, axis=0, keepdims=True)
        y = hsum * lax.rsqrt(ms + EPS) * og
        return y.T.astype(BF16)

    if ctx_out:
        for c in range(n_c):
            yc_ref[0, c * L:(c + 1) * L, lanes] = outputs(
                c, kc_ref[0, hh, c * L:(c + 1) * L, :], qtc_ref[0, hh, c], vtc_ref[0, hh, c],
                otc_ref[0, hh, c])
            tick(boundary=True)
    else:
        yc_ref[0, :, lanes] = jnp.zeros((yc_ref.shape[1], HEAD_DIM), yc_ref.dtype)

    for c in range(n_l):
        yl_ref[0, c * L:(c + 1) * L, lanes] = outputs(
            n_c + c, kl_ref[0, hh, c * L:(c + 1) * L, :], qtl_ref[0, hh, c], vtl_ref[0, hh, c],
            otl_ref[0, hh, c])
        tick(boundary=True)


def _mlstm_call(kc, qtc, vtc, otc, kl, qtl, vtl, otl, gates, z, w_conv, *, ctx_out):
    b = kl.shape[0]
    tc, tl = kc.shape[2], kl.shape[2]
    n_c, n_l, L = qtc.shape[2], qtl.shape[2], qtl.shape[4]
    n_ch = n_c + n_l
    assert L == LANES and n_ch <= L

    pairs = N_HEADS // 2

    def nat(t):
        return pl.BlockSpec((1, 2, t, HEAD_DIM), lambda i, j: (i, j, 0, 0))

    def tr(n):
        return pl.BlockSpec((1, 2, n, HEAD_DIM, L), lambda i, j: (i, j, 0, 0, 0))

    def nxt(*tail):
        def index(i, j):
            f = jnp.minimum(i * pairs + j + 1, b * pairs - 1)
            return (f // pairs, 2 * (f % pairs)) + tail
        return index

    def nat_next(t):
        return pl.BlockSpec((1, 1, t, HEAD_DIM), nxt(0, 0))

    def tr_next(n):
        return pl.BlockSpec((1, 1, n, HEAD_DIM, L), nxt(0, 0, 0))

    scratch_set = [pltpu.VMEM((2, n_ch, 8, L), F32),
                   pltpu.VMEM((2, L, LANES), F32),
                   pltpu.VMEM((2, n_ch, STATE_ROWS, HEAD_DIM), F32),
                   pltpu.VMEM((2, n_ch, STATE_ROWS, HEAD_DIM), BF16)]
    assert D_CONV == 2 * pairs * LANES and tl % GRID_W == 0
    zblk = (1, tl, LANES)
    half = D_CONV // 2
    return pl.pallas_call(
        functools.partial(_mlstm_kernel, ctx_out=ctx_out),
        grid=(b, pairs),
        in_specs=[nat(tc), tr(n_c), tr(n_c), tr(n_c), nat(tl), tr(n_l), tr(n_l), tr(n_l),
                  pl.BlockSpec((1, 2, 4, n_ch, L), lambda i, j: (i, j, 0, 0, 0)),
                  nat_next(tc), tr_next(n_c), nat_next(tl), tr_next(n_l),
                  pl.BlockSpec((1, 1, 4, n_ch, L), nxt(0, 0, 0)),
                  pl.BlockSpec(zblk, lambda i, j: (i, 0, j)),
                  pl.BlockSpec(zblk, lambda i, j: (i, 0, pairs + j)),
                  pl.BlockSpec((CONV_WIDTH, LANES), lambda i, j: (0, j)),
                  pl.BlockSpec((CONV_WIDTH, LANES), lambda i, j: (0, pairs + j))],
        out_specs=[pl.BlockSpec((1, tc, 2 * HEAD_DIM), lambda i, j: (i, 0, j)),
                   pl.BlockSpec((1, tl, 2 * HEAD_DIM), lambda i, j: (i, 0, j)),
                   pl.BlockSpec(zblk, lambda i, j: (i, 0, j)),
                   pl.BlockSpec(zblk, lambda i, j: (i, 0, j))],
        out_shape=[jax.ShapeDtypeStruct((b, tc, D_MLSTM), BF16),
                   jax.ShapeDtypeStruct((b, tl, D_MLSTM), BF16),
                   jax.ShapeDtypeStruct((b, tl, half), F32),
                   jax.ShapeDtypeStruct((b, tl, half), F32)],
        scratch_shapes=scratch_set + scratch_set
                       + [pltpu.VMEM(((tl // GRID_W) * (GRID_W + 2 * SEG_LPAD), LANES), F32)],
        compiler_params=_cparams(2),
        name="mlstm",
    )(kc, qtc, vtc, otc, kl, qtl, vtl, otl, gates, kc, vtc, kl, vtl, gates, z, z, w_conv, w_conv)


SEG_LPAD = 16
ROW_TILE = 64


def _conv_seg_block(z_ref, w_ref, o_ref, zp_ref, cols, seg):
    t = z_ref.shape[1]
    pitch = seg + 2 * SEG_LPAD
    n_seg = t // seg
    zp_ref[0:n_seg * pitch, :] = jnp.zeros((n_seg * pitch, LANES), F32)
    for s in range(n_seg):
        zp_ref[s * pitch + SEG_LPAD:s * pitch + SEG_LPAD + seg, :] = z_ref[0, s * seg:(s + 1) * seg, cols]
    tiles = seg // ROW_TILE
    wrows = [w_ref[j:j + 1, cols] for j in range(CONV_WIDTH)]

    def tile(i, carry):
        s, q = i // tiles, i % tiles
        src = s * pitch + q * ROW_TILE + (SEG_LPAD - CONV_PAD)
        acc = wrows[0] * zp_ref[pl.ds(src, ROW_TILE), :]
        for j in range(1, CONV_WIDTH):
            acc = acc + wrows[j] * zp_ref[pl.ds(src + j, ROW_TILE), :]
        o_ref[0, pl.ds(pl.multiple_of(i * ROW_TILE, ROW_TILE), ROW_TILE), cols] = acc
        return carry

    lax.fori_loop(0, t // ROW_TILE, tile, 0)


def _conv_rows_block(z_ref, w_ref, o_ref, zp_ref, cols):
    t = z_ref.shape[1]
    pad = CONV_PAD * ROW_TILE
    zp_ref[0:pad, :] = jnp.zeros((pad, LANES), F32)
    zp_ref[pad + t:pad + t + pad, :] = jnp.zeros((pad, LANES), F32)
    zp_ref[pad:pad + t, :] = z_ref[0, :, cols]
    wrows = [w_ref[j:j + 1, cols] for j in range(CONV_WIDTH)]

    def tile(i, carry):
        base = pl.multiple_of(i * ROW_TILE, ROW_TILE)
        acc = wrows[0] * zp_ref[pl.ds(base, ROW_TILE), :]
        for j in range(1, CONV_WIDTH):
            acc = acc + wrows[j] * zp_ref[pl.ds(base + j * ROW_TILE, ROW_TILE), :]
        o_ref[0, pl.ds(base, ROW_TILE), cols] = acc
        return carry

    lax.fori_loop(0, t // ROW_TILE, tile, 0)


def _conv_kernel(z_ref, w_ref, o_ref, zp_ref, *, plan):
    for blk, seg in enumerate(plan):
        cols = slice(blk * LANES, (blk + 1) * LANES)
        if seg is None:
            _conv_rows_block(z_ref, w_ref, o_ref, zp_ref, cols)
        else:
            _conv_seg_block(z_ref, w_ref, o_ref, zp_ref, cols, seg)


def _conv_call(z, w, plan):
    b, t, ch = z.shape
    assert ROW_TILE == GRID_W and len(plan) * LANES == ch
    rows = max((t + 2 * CONV_PAD * ROW_TILE) if seg is None else (t // seg) * (seg + 2 * SEG_LPAD)
               for seg in plan)
    return pl.pallas_call(
        functools.partial(_conv_kernel, plan=plan),
        grid=(b,),
        in_specs=[pl.BlockSpec((1, t, ch), lambda i: (i, 0, 0)),
                  pl.BlockSpec((CONV_WIDTH, ch), lambda i: (0, 0))],
        out_specs=pl.BlockSpec((1, t, ch), lambda i: (i, 0, 0)),
        out_shape=jax.ShapeDtypeStruct((b, t, ch), F32),
        scratch_shapes=[pltpu.VMEM((rows, LANES), F32)],
        compiler_params=_cparams(1),
        name="conv",
    )(z, w)


def _gate_rows(g, chunk):
    b, _, t = g.shape
    return g.reshape(b, 4, N_HEADS, t // chunk, chunk).transpose(0, 2, 1, 3, 4)


def kernel(x, c, ctx, c_ctx, w_mod, b_mod, norm_ffn1, ffn1_up, ffn1_down, norm_mix, w_in, b_in,
           mlstm_norm, conv_w, conv_b, conv_norm_g, conv_norm_b, w_out, norm_ffn2, ffn2_up,
           ffn2_down, norm_final):
    b, s, d = x.shape
    depth = w_mod.shape[0]
    d_ff = ffn1_down.shape[1]
    nck = d_ff // FF_CHUNK
    tm_c = ctx.shape[1]
    L = MLSTM_CHUNK

    def flat(a):
        return a.reshape(1, -1, a.shape[-1])

    rows = -(-(b + 1) // 8) * 8
    c_all = jnp.concatenate([c, c_ctx[None, :], jnp.zeros((rows - b - 1, d), F32)], axis=0)
    mods_all = _mod_call(c_all, w_mod, b_mod).reshape(depth, rows, N_MOD, 1, d)

    q0, k0, v0, o0 = 0, D_MLSTM, 2 * D_MLSTM, 3 * D_MLSTM
    g0 = 4 * D_MLSTM
    u0 = g0 + 4 * N_HEADS

    xc = ctx
    for l in range(depth):
        last = l == depth - 1
        mods = mods_all[l]

        def ffn_weights(up, down):
            return up.astype(BF16), down.reshape(nck, FF_CHUNK, d).astype(BF16)

        w1 = ffn_weights(ffn1_up[l], ffn1_down[l])
        x = _ffn_call(x, mods, norm_ffn1[l], *w1, js=0, mod_row=None, tm=FFN_TILE)
        xc = _ffn_call(flat(xc), mods, norm_ffn1[l], *w1, js=0, mod_row=b, tm=FFN_TILE).reshape(xc.shape)

        wi, bi = w_in[l], b_in[l]
        wn = jnp.concatenate([wi[:, k0:v0], wi[:, u0:]], axis=1).astype(BF16)
        bn = jnp.concatenate([bi[k0:v0], bi[u0:]]).reshape(1, -1)
        wt = jnp.concatenate([wi[:, q0:k0], wi[:, v0:u0]], axis=1).T.astype(BF16)
        bt = jnp.broadcast_to(jnp.concatenate([bi[q0:k0], bi[v0:u0]])[:, None],
                              (wt.shape[0], LANES))
        gain = jnp.broadcast_to(mlstm_norm[l][:, None], (D_MLSTM, LANES))
        kl, zl, gl, qtl, vtl, otl = _proj_call(x, mods, norm_mix[l], wn, bn, wt, bt, gain,
                                               mod_row=None, tm=PROJ_TILE, chunk=L)
        kc, zc, gc, qtc, vtc, otc = _proj_call(xc, mods, norm_mix[l], wn, bn, wt, bt, gain,
                                               mod_row=b, tm=tm_c, chunk=L)

        gates = jnp.concatenate([_gate_rows(gc, L), _gate_rows(gl, L)], axis=3)
        ymc, yml, z_cols, z_rows = _mlstm_call(kc, qtc, vtc, otc, kl, qtl, vtl, otl, gates,
                                               zl, conv_w[l], ctx_out=not last)
        nblk = D_CONV // LANES
        wm = w_out[l][:D_MLSTM].astype(BF16)
        wc = w_out[l][D_MLSTM:].astype(BF16)
        mix = (conv_b[l], conv_norm_g[l], conv_norm_b[l], wm, wc)

        w2 = ffn_weights(ffn2_up[l], ffn2_down[l])
        x = _ffn_call(x, mods, norm_ffn2[l], *w2, js=6, mod_row=None, tm=FFN_TILE,
                      final_g=norm_final if last else None, mix=(yml, [z_cols, z_rows], *mix))
        if not last:
            z_ctx = _conv_call(zc, conv_w[l], (zc.shape[1],) * nblk)
            xc = _ffn_call(flat(xc), mods, norm_ffn2[l], *w2, js=6, mod_row=b, tm=FFN_TILE,
                           mix=(flat(ymc), [flat(z_ctx)], *mix)).reshape(xc.shape)
    return x
```

```python
import functools

import jax
import jax.numpy as jnp
from jax import lax
from jax.experimental import pallas as pl
from jax.experimental.pallas import tpu as pltpu

F32 = jnp.float32
BF16 = jnp.bfloat16

GRID_W = 64
N_HEADS = 4
HEAD_DIM = 128
D_MLSTM = N_HEADS * HEAD_DIM
D_CONV = 512
CONV_WIDTH = 31
CONV_PAD = (CONV_WIDTH - 1) // 2
N_MOD = 9
EPS = 1e-6

LANES = 128
FF_CHUNK = 256
FFN_TILE = 1024
PROJ_TILE = 1024
EDGE_SLICES = 4
FFN_UNROLL = 10
MLSTM_CHUNK = 128
STATE_ROWS = HEAD_DIM + 16
CONV_STAGES_PER_CHUNK = 3
NEG_BIG = -1e30
LOG2E = 1.4426950408889634
VMEM_LIMIT = 56 * 1024 * 1024


def _cparams(n_axes):
    return pltpu.CompilerParams(
        dimension_semantics=("arbitrary",) * n_axes, vmem_limit_bytes=VMEM_LIMIT)


def _resident(shape):
    nd = len(shape)
    return pl.BlockSpec(shape, lambda *_: (0,) * nd, pipeline_mode=pl.Buffered(1))


def _rms_mod(x, g, shift, scale):
    ms = jnp.mean(x * x, axis=-1, keepdims=True)
    return (x * lax.rsqrt(ms + EPS)) * (g * (1.0 + scale)) + shift


def _mod_kernel(c_ref, w_ref, b_ref, o_ref):
    c = c_ref[...]
    s = (c * jax.nn.sigmoid(c)).astype(BF16)
    o_ref[0] = jnp.dot(s, w_ref[0].astype(BF16), preferred_element_type=F32) + b_ref[0]


def _mod_call(c_all, w_mod, b_mod):
    depth, d, n = w_mod.shape
    r = c_all.shape[0]
    tn = n // 8
    return pl.pallas_call(
        _mod_kernel,
        grid=(depth, n // tn),
        in_specs=[pl.BlockSpec((r, d), lambda l, j: (0, 0)),
                  pl.BlockSpec((1, d, tn), lambda l, j: (l, 0, j)),
                  pl.BlockSpec((1, 1, tn), lambda l, j: (l, 0, j))],
        out_specs=pl.BlockSpec((1, r, tn), lambda l, j: (l, 0, j)),
        out_shape=jax.ShapeDtypeStruct((depth, r, n), F32),
        compiler_params=_cparams(2),
        name="mod",
    )(c_all, w_mod, b_mod.reshape(depth, 1, n))


def _ffn_kernel(*refs, js, final, nz):
    x_ref, mod_ref = refs[:2]
    refs = refs[2:]
    if nz:
        ym_ref, z_refs = refs[0], refs[1:1 + nz]
        cb_ref, cg_ref, cnb_ref, wm_ref, wc_ref = refs[1 + nz:6 + nz]
        refs = refs[6 + nz:]
    ng_ref, wup_ref, wd_ref = refs[:3]
    refs = refs[3:]
    if final:
        gf_ref, refs = refs[0], refs[1:]
    o_ref, h_ref, a_ref, acc_ref = refs
    nck, fc, _ = wd_ref.shape
    d_ff = nck * fc

    tm = x_ref.shape[1]
    edge_rows = tm // EDGE_SLICES

    def up(c, rows=slice(None)):
        h = h_ref[rows, :]
        col = c * fc if isinstance(c, int) else pl.multiple_of(c * fc, fc)
        g = jnp.dot(h, wup_ref[:, pl.ds(col, fc)], preferred_element_type=F32)
        u = jnp.dot(h, wup_ref[:, pl.ds(d_ff + col, fc)], preferred_element_type=F32)
        return (g * jax.nn.sigmoid(g) * u).astype(BF16)

    for r in range(EDGE_SLICES):
        rows = slice(r * edge_rows, (r + 1) * edge_rows)
        x = x_ref[0, rows, :]
        if nz:
            zs = [zr[0, rows, :] for zr in z_refs]
            z = (jnp.concatenate(zs, axis=-1) if nz > 1 else zs[0]) + cb_ref[...]
            zc = z - jnp.mean(z, axis=-1, keepdims=True)
            var = jnp.mean(zc * zc, axis=-1, keepdims=True)
            ln = zc * lax.rsqrt(var + EPS) * cg_ref[...] + cnb_ref[...]
            yc = (ln * jax.nn.sigmoid(ln)).astype(BF16)
            y = (jnp.dot(ym_ref[0, rows, :], wm_ref[...], preferred_element_type=F32)
                 + jnp.dot(yc, wc_ref[...], preferred_element_type=F32))
            x = x + mod_ref[0, 5] * y
        o_ref[0, rows, :] = x
        h_ref[rows, :] = _rms_mod(x, ng_ref[...], mod_ref[0, js], mod_ref[0, js + 1]).astype(BF16)
        a_ref[0, rows, :] = up(0, rows)

    acc_ref[...] = jnp.zeros_like(acc_ref)

    def chunk(c, carry):
        a_prev = a_ref[(c - 1) % 2]
        a_ref[c % 2] = up(c)
        acc_ref[...] += jnp.dot(a_prev, wd_ref[c - 1], preferred_element_type=F32)
        return carry

    lax.fori_loop(1, nck, chunk, 0, unroll=FFN_UNROLL)
    for r in range(EDGE_SLICES):
        rows = slice(r * edge_rows, (r + 1) * edge_rows)
        d_last = jnp.dot(a_ref[(nck - 1) % 2, rows, :], wd_ref[nck - 1], preferred_element_type=F32)
        out = o_ref[0, rows, :] + (0.5 * mod_ref[0, js + 2]) * (acc_ref[rows, :] + d_last)
        if final:
            ms = jnp.mean(out * out, axis=-1, keepdims=True)
            out = out * lax.rsqrt(ms + EPS) * gf_ref[...]
        o_ref[0, rows, :] = out


def _ffn_call(x, mods, ng, wup, wd, *, js, mod_row, tm, final_g=None, mix=None):
    b, t, d = x.shape
    fc = wd.shape[1]
    row = (lambda i, j: (i, 0, 0, 0)) if mod_row is None else (lambda i, j: (mod_row, 0, 0, 0))

    def tok(c):
        return pl.BlockSpec((1, tm, c), lambda i, j: (i, j, 0))

    in_specs = [tok(d), pl.BlockSpec((1, N_MOD, 1, d), row)]
    args = [x, mods]
    nz = 0
    if mix is not None:
        ym, zs, cb, cg, cnb, wm, wc = mix
        nz = len(zs)
        in_specs += [tok(ym.shape[2])] + [tok(z.shape[2]) for z in zs]
        in_specs += [_resident((1, D_CONV))] * 3 + [_resident(wm.shape), _resident(wc.shape)]
        args += [ym, *zs, cb.reshape(1, -1), cg.reshape(1, -1), cnb.reshape(1, -1), wm, wc]
    in_specs += [_resident((1, d)), _resident(wup.shape), _resident(wd.shape)]
    args += [ng.reshape(1, d), wup, wd]
    if final_g is not None:
        in_specs.append(_resident((1, d)))
        args.append(final_g.reshape(1, d))
    return pl.pallas_call(
        functools.partial(_ffn_kernel, js=js, final=final_g is not None, nz=nz),
        grid=(b, t // tm),
        in_specs=in_specs,
        out_specs=tok(d),
        out_shape=jax.ShapeDtypeStruct((b, t, d), F32),
        scratch_shapes=[pltpu.VMEM((tm, d), BF16), pltpu.VMEM((2, tm, fc), BF16),
                        pltpu.VMEM((tm, d), F32)],
        compiler_params=_cparams(2),
        name="ffn",
    )(*args)


def _proj_kernel(x_ref, mod_ref, ng_ref, wn_ref, bn_ref, wt_ref, bt_ref, gain_ref,
                 k_ref, z_ref, g_ref, qt_ref, vt_ref, ot_ref, *, chunk):
    tm = x_ref.shape[1]
    u = _rms_mod(x_ref[0], ng_ref[...], mod_ref[0, 3], mod_ref[0, 4]).astype(BF16)
    pn = jnp.dot(u, wn_ref[...], preferred_element_type=F32) + bn_ref[...]
    for h in range(N_HEADS):
        k_ref[0, h] = pn[:, h * HEAD_DIM:(h + 1) * HEAD_DIM].astype(BF16)
    a = pn[:, D_MLSTM:D_MLSTM + D_CONV]
    gl = pn[:, D_MLSTM + D_CONV:D_MLSTM + 2 * D_CONV]
    z_ref[0] = a * jax.nn.sigmoid(gl)
    pt = lax.dot_general(wt_ref[...], u, (((1,), (1,)), ((), ())), preferred_element_type=F32)
    rg = slice(3 * D_MLSTM, 3 * D_MLSTM + 4 * N_HEADS)
    for lb in range(tm // LANES):
        cols = slice(lb * LANES, (lb + 1) * LANES)
        g_ref[0, :, cols] = pt[rg, cols] + bt_ref[rg, :]
    scale = HEAD_DIM ** -0.5
    for h in range(N_HEADS):
        for lb in range(tm // LANES):
            c, off = divmod(lb * LANES, chunk)
            cols = slice(lb * LANES, (lb + 1) * LANES)
            dst = (0, h, c, slice(None), slice(off, off + LANES))
            rq = slice(h * HEAD_DIM, (h + 1) * HEAD_DIM)
            rv = slice(D_MLSTM + h * HEAD_DIM, D_MLSTM + (h + 1) * HEAD_DIM)
            ro = slice(2 * D_MLSTM + h * HEAD_DIM, 2 * D_MLSTM + (h + 1) * HEAD_DIM)
            qt_ref[dst] = ((pt[rq, cols] + bt_ref[rq, :]) * scale).astype(BF16)
            vt_ref[dst] = (pt[rv, cols] + bt_ref[rv, :]).astype(BF16)
            ot_ref[dst] = jax.nn.sigmoid(pt[ro, cols] + bt_ref[ro, :]) * gain_ref[rq, :]


def _proj_call(x, mods, ng, wn, bn, wt, bt, gain, *, mod_row, tm, chunk):
    b, t, d = x.shape
    nch, cpt = t // chunk, tm // chunk
    row = (lambda i, j: (i, 0, 0, 0)) if mod_row is None else (lambda i, j: (mod_row, 0, 0, 0))
    tspec = pl.BlockSpec((1, N_HEADS, cpt, HEAD_DIM, chunk), lambda i, j: (i, 0, j, 0, 0))
    tshape = (b, N_HEADS, nch, HEAD_DIM, chunk)
    return pl.pallas_call(
        functools.partial(_proj_kernel, chunk=chunk),
        grid=(b, t // tm),
        in_specs=[pl.BlockSpec((1, tm, d), lambda i, j: (i, j, 0)),
                  pl.BlockSpec((1, N_MOD, 1, d), row),
                  _resident((1, d)), _resident(wn.shape), _resident(bn.shape),
                  _resident(wt.shape), _resident(bt.shape), _resident(gain.shape)],
        out_specs=[pl.BlockSpec((1, N_HEADS, tm, HEAD_DIM), lambda i, j: (i, 0, j, 0)),
                   pl.BlockSpec((1, tm, D_CONV), lambda i, j: (i, j, 0)),
                   pl.BlockSpec((1, 4 * N_HEADS, tm), lambda i, j: (i, 0, j)),
                   tspec, tspec, tspec],
        out_shape=[jax.ShapeDtypeStruct((b, N_HEADS, t, HEAD_DIM), BF16),
                   jax.ShapeDtypeStruct((b, t, D_CONV), F32),
                   jax.ShapeDtypeStruct((b, 4 * N_HEADS, t), F32),
                   jax.ShapeDtypeStruct(tshape, BF16),
                   jax.ShapeDtypeStruct(tshape, BF16),
                   jax.ShapeDtypeStruct(tshape, F32)],
        compiler_params=_cparams(2),
        name="proj",
    )(x, mods, ng.reshape(1, d), wn, bn, wt, bt, gain)


def _mlstm_state_pass(g_ref, kc_ref, vtc_ref, kl_ref, vtl_ref, hh, scratch, masks, orders):
    gs_ref, rt_ref, upd_ref, st_ref = scratch
    _, n_ch, L = g_ref.shape[2:]
    for d in range(2):
        li, fpre = g_ref[0, hh, 2 * d], g_ref[0, hh, 2 * d + 1]
        lf = jnp.minimum(fpre, 0.0) - jnp.log1p(jnp.exp(-jnp.abs(fpre)))
        tri = masks[d].astype(F32)
        cum = jnp.dot(lf, tri, preferred_element_type=F32, precision=lax.Precision.HIGHEST)
        edge = L - 1 if d == 0 else 0
        total = cum[:, edge:edge + 1]
        r = li - cum
        yield
        pm = r
        lane = lax.broadcasted_iota(jnp.int32, r.shape, 1)
        s = 1
        while s < L:
            if d == 1:
                sh = jnp.where(lane < L - s, pltpu.roll(pm, L - s, 1), NEG_BIG)
            else:
                sh = jnp.where(lane >= s, pltpu.roll(pm, s, 1), NEG_BIG)
            pm = jnp.maximum(pm, sh)
            s *= 2
            yield
        rmax = pm[:, edge:edge + 1]
        m = jnp.zeros((1, 1), F32)
        m_prev, m_new = [None] * n_ch, [None] * n_ch
        for c in orders[d]:
            m_prev[c] = m
            m = total[c:c + 1] + jnp.maximum(m, rmax[c:c + 1])
            m_new[c] = m
        yield
        m_prev = jnp.concatenate(m_prev, axis=0)
        m_new = jnp.concatenate(m_new, axis=0)
        mx = jnp.maximum(m_prev, pm)
        alpha = -LOG2E * mx
        inter = jnp.exp(m_prev - mx)
        clampv = jnp.exp(-(cum + mx))
        wk = jnp.exp(total + r - m_new)
        decay = jnp.broadcast_to(jnp.exp(total + m_prev - m_new), (n_ch, L))
        rt_ref[d] = jnp.concatenate([LOG2E * r, jnp.zeros((LANES - n_ch, L), F32)], axis=0).T
        pad = jnp.zeros((3, L), F32)
        for c in range(n_ch):
            gs_ref[d, c] = jnp.concatenate(
                [alpha[c:c + 1], inter[c:c + 1], clampv[c:c + 1], wk[c:c + 1], decay[c:c + 1], pad],
                axis=0)
        yield

    n_c = kc_ref.shape[2] // L
    for c in range(n_ch):
        if c < n_c:
            k, vt = kc_ref[0, hh, c * L:(c + 1) * L, :], vtc_ref[0, hh, c]
        else:
            k, vt = kl_ref[0, hh, (c - n_c) * L:(c - n_c + 1) * L, :], vtl_ref[0, hh, c - n_c]
        vtf = vt.astype(F32)
        parts = []
        for d in range(2):
            wk = gs_ref[d, c][3:4]
            parts += [vtf * wk, wk, jnp.zeros((STATE_ROWS - HEAD_DIM - 1, L), F32)]
        lhs = jnp.concatenate(parts, axis=0).astype(BF16)
        upd = jnp.dot(lhs, k, preferred_element_type=F32)
        upd_ref[0, c] = upd[:STATE_ROWS]
        upd_ref[1, c] = upd[STATE_ROWS:]
        yield

    states = [jnp.zeros((STATE_ROWS, HEAD_DIM), F32)] * 2
    for i in range(n_ch):
        for d in range(2):
            c = orders[d][i]
            st_ref[d, c] = states[d].astype(BF16)
            states[d] = gs_ref[d, c][4:5, :HEAD_DIM] * states[d] + upd_ref[d, c]
        yield


def _conv_cols_pass(z_ref, w_ref, o_ref, zp_ref):
    seg, pitch = GRID_W, GRID_W + 2 * SEG_LPAD
    n_seg = z_ref.shape[1] // seg
    zp_ref[...] = jnp.zeros_like(zp_ref)
    yield
    for s in range(n_seg):
        zp_ref[s * pitch + SEG_LPAD:s * pitch + SEG_LPAD + seg, :] = z_ref[0, s * seg:(s + 1) * seg, :]
        if s % 8 == 7:
            yield
    for s in range(n_seg):
        src = s * pitch + SEG_LPAD - CONV_PAD
        acc = w_ref[0:1, :] * zp_ref[src:src + seg, :]
        for j in range(1, CONV_WIDTH):
            acc = acc + w_ref[j:j + 1, :] * zp_ref[src + j:src + j + seg, :]
        o_ref[0, s * seg:(s + 1) * seg, :] = acc
        yield


def _conv_rows_pass(z_ref, w_ref, o_ref):
    n = z_ref.shape[1] // GRID_W
    for i in range(n):
        acc = None
        for j in range(CONV_WIDTH):
            src = i + j - CONV_PAD
            if 0 <= src < n:
                term = w_ref[j:j + 1, :] * z_ref[0, src * GRID_W:(src + 1) * GRID_W, :]
                acc = term if acc is None else acc + term
        o_ref[0, i * GRID_W:(i + 1) * GRID_W, :] = acc
        yield


def _mlstm_kernel(kc_ref, qtc_ref, vtc_ref, otc_ref, kl_ref, qtl_ref, vtl_ref, otl_ref, g_ref,
                  kcn_ref, vtcn_ref, kln_ref, vtln_ref, gn_ref, zcol_ref, zrow_ref, wcol_ref, wrow_ref,
                  yc_ref, yl_ref, ocol_ref, orow_ref, *scratch, ctx_out):
    n_c, n_l = qtc_ref.shape[2], qtl_ref.shape[2]
    n_ch = n_c + n_l
    L = qtl_ref.shape[4]
    srow = lax.broadcasted_iota(jnp.int32, (L, L), 0)
    tcol = lax.broadcasted_iota(jnp.int32, (L, L), 1)
    masks = (srow <= tcol, srow >= tcol)
    orders = (list(range(n_ch)),
              list(range(n_c - 1, -1, -1)) + list(range(n_ch - 1, n_c - 1, -1)))
    sets = (scratch[:4], scratch[4:8])
    zp_ref = scratch[8]
    cur_refs = (g_ref, kc_ref, vtc_ref, kl_ref, vtl_ref)
    next_refs = (gn_ref, kcn_ref, vtcn_ref, kln_ref, vtln_ref)

    @pl.when((pl.program_id(0) == 0) & (pl.program_id(1) == 0))
    def _():
        for _ in _mlstm_state_pass(*cur_refs, 0, sets[0], masks, orders):
            pass

    for hh in range(2):
        if hh == 0:
            ahead = _mlstm_state_pass(*cur_refs, 1, sets[1], masks, orders)
            conv = _conv_cols_pass(zcol_ref, wcol_ref, ocol_ref, zp_ref)
        else:
            ahead = _mlstm_state_pass(*next_refs, 0, sets[0], masks, orders)
            conv = _conv_rows_pass(zrow_ref, wrow_ref, orow_ref)
        _mlstm_outputs(kc_ref, qtc_ref, vtc_ref, otc_ref, kl_ref, qtl_ref, vtl_ref, otl_ref,
                       yc_ref, yl_ref, hh, sets[hh], masks, (ahead, conv), ctx_out)
        for gen in (ahead, conv):
            for _ in gen:
                pass


def _mlstm_outputs(kc_ref, qtc_ref, vtc_ref, otc_ref, kl_ref, qtl_ref, vtl_ref, otl_ref,
                   yc_ref, yl_ref, hh, scratch, masks, fillers, ctx_out):
    gs_ref, rt_ref, _, st_ref = scratch
    n_c, n_l = qtc_ref.shape[2], qtl_ref.shape[2]
    L = qtl_ref.shape[4]
    lanes = slice(hh * HEAD_DIM, (hh + 1) * HEAD_DIM)
    ahead, conv = fillers

    def tick(boundary=False):
        next(ahead, None)
        if boundary:
            for _ in range(CONV_STAGES_PER_CHUNK):
                next(conv, None)

    def outputs(cidx, k, qt, vt, og):
        s_t = jnp.dot(k, qt, preferred_element_type=F32)
        carried = jnp.dot(jnp.concatenate([st_ref[0, cidx], st_ref[1, cidx]], axis=0), qt,
                          preferred_element_type=F32)
        tick()
        sws, dens, rows = [], [], []
        for d in range(2):
            gsl = gs_ref[d, cidx]
            rows.append(gsl)
            arg = jnp.where(masks[d], rt_ref[d, :, cidx:cidx + 1] + gsl[0:1], NEG_BIG)
            sw = s_t * jnp.exp2(arg)
            sws.append(sw.astype(BF16))
            qn = carried[d * STATE_ROWS + HEAD_DIM:d * STATE_ROWS + HEAD_DIM + 1]
            dens.append(jnp.sum(sw, axis=0, keepdims=True) + gsl[1:2] * qn)
            tick()
        num2 = jnp.dot(vt, jnp.concatenate(sws, axis=1), preferred_element_type=F32)
        tick()
        hsum = None
        for d in range(2):
            num = (num2[:, d * L:(d + 1) * L]
                   + rows[d][1:2] * carried[d * STATE_ROWS:d * STATE_ROWS + HEAD_DIM])
            h = num * (1.0 / jnp.maximum(jnp.abs(dens[d]), rows[d][2:3]))
            hsum = h if hsum is None else hsum + h
        ms = jnp.mean(hsum * hsum, axis=0, keepdims=True)
        y = hsum * lax.rsqrt(ms + EPS) * og
        return y.T.astype(BF16)

    if ctx_out:
        for c in range(n_c):
            yc_ref[0, c * L:(c + 1) * L, lanes] = outputs(
                c, kc_ref[0, hh, c * L:(c + 1) * L, :], qtc_ref[0, hh, c], vtc_ref[0, hh, c],
                otc_ref[0, hh, c])
            tick(boundary=True)
    else:
        yc_ref[0, :, lanes] = jnp.zeros((yc_ref.shape[1], HEAD_DIM), yc_ref.dtype)

    for c in range(n_l):
        yl_ref[0, c * L:(c + 1) * L, lanes] = outputs(
            n_c + c, kl_ref[0, hh, c * L:(c + 1) * L, :], qtl_ref[0, hh, c], vtl_ref[0, hh, c],
            otl_ref[0, hh, c])
        tick(boundary=True)


def _mlstm_call(kc, qtc, vtc, otc, kl, qtl, vtl, otl, gates, z, w_conv, *, ctx_out):
    b = kl.shape[0]
    tc, tl = kc.shape[2], kl.shape[2]
    n_c, n_l, L = qtc.shape[2], qtl.shape[2], qtl.shape[4]
    n_ch = n_c + n_l
    assert L == LANES and n_ch <= L

    pairs = N_HEADS // 2

    def nat(t):
        return pl.BlockSpec((1, 2, t, HEAD_DIM), lambda i, j: (i, j, 0, 0))

    def tr(n):
        return pl.BlockSpec((1, 2, n, HEAD_DIM, L), lambda i, j: (i, j, 0, 0, 0))

    def nxt(*tail):
        def index(i, j):
            f = jnp.minimum(i * pairs + j + 1, b * pairs - 1)
            return (f // pairs, 2 * (f % pairs)) + tail
        return index

    def nat_next(t):
        return pl.BlockSpec((1, 1, t, HEAD_DIM), nxt(0, 0))

    def tr_next(n):
        return pl.BlockSpec((1, 1, n, HEAD_DIM, L), nxt(0, 0, 0))

    scratch_set = [pltpu.VMEM((2, n_ch, 8, L), F32),
                   pltpu.VMEM((2, L, LANES), F32),
                   pltpu.VMEM((2, n_ch, STATE_ROWS, HEAD_DIM), F32),
                   pltpu.VMEM((2, n_ch, STATE_ROWS, HEAD_DIM), BF16)]
    assert D_CONV == 2 * pairs * LANES and tl % GRID_W == 0
    zblk = (1, tl, LANES)
    half = D_CONV // 2
    return pl.pallas_call(
        functools.partial(_mlstm_kernel, ctx_out=ctx_out),
        grid=(b, pairs),
        in_specs=[nat(tc), tr(n_c), tr(n_c), tr(n_c), nat(tl), tr(n_l), tr(n_l), tr(n_l),
                  pl.BlockSpec((1, 2, 4, n_ch, L), lambda i, j: (i, j, 0, 0, 0)),
                  nat_next(tc), tr_next(n_c), nat_next(tl), tr_next(n_l),
                  pl.BlockSpec((1, 1, 4, n_ch, L), nxt(0, 0, 0)),
                  pl.BlockSpec(zblk, lambda i, j: (i, 0, j)),
                  pl.BlockSpec(zblk, lambda i, j: (i, 0, pairs + j)),
                  pl.BlockSpec((CONV_WIDTH, LANES), lambda i, j: (0, j)),
                  pl.BlockSpec((CONV_WIDTH, LANES), lambda i, j: (0, pairs + j))],
        out_specs=[pl.BlockSpec((1, tc, 2 * HEAD_DIM), lambda i, j: (i, 0, j)),
                   pl.BlockSpec((1, tl, 2 * HEAD_DIM), lambda i, j: (i, 0, j)),
                   pl.BlockSpec(zblk, lambda i, j: (i, 0, j)),
                   pl.BlockSpec(zblk, lambda i, j: (i, 0, j))],
        out_shape=[jax.ShapeDtypeStruct((b, tc, D_MLSTM), BF16),
                   jax.ShapeDtypeStruct((b, tl, D_MLSTM), BF16),
                   jax.ShapeDtypeStruct((b, tl, half), F32),
                   jax.ShapeDtypeStruct((b, tl, half), F32)],
        scratch_shapes=scratch_set + scratch_set
                       + [pltpu.VMEM(((tl // GRID_W) * (GRID_W + 2 * SEG_LPAD), LANES), F32)],
        compiler_params=_cparams(2),
        name="mlstm",
    )(kc, qtc, vtc, otc, kl, qtl, vtl, otl, gates, kc, vtc, kl, vtl, gates, z, z, w_conv, w_conv)


SEG_LPAD = 16
ROW_TILE = 64


def _conv_seg_block(z_ref, w_ref, o_ref, zp_ref, cols, seg):
    t = z_ref.shape[1]
    pitch = seg + 2 * SEG_LPAD
    n_seg = t // seg
    zp_ref[0:n_seg * pitch, :] = jnp.zeros((n_seg * pitch, LANES), F32)
    for s in range(n_seg):
        zp_ref[s * pitch + SEG_LPAD:s * pitch + SEG_LPAD + seg, :] = z_ref[0, s * seg:(s + 1) * seg, cols]
    tiles = seg // ROW_TILE
    wrows = [w_ref[j:j + 1, cols] for j in range(CONV_WIDTH)]

    def tile(i, carry):
        s, q = i // tiles, i % tiles
        src = s * pitch + q * ROW_TILE + (SEG_LPAD - CONV_PAD)
        acc = wrows[0] * zp_ref[pl.ds(src, ROW_TILE), :]
        for j in range(1, CONV_WIDTH):
            acc = acc + wrows[j] * zp_ref[pl.ds(src + j, ROW_TILE), :]
        o_ref[0, pl.ds(pl.multiple_of(i * ROW_TILE, ROW_TILE), ROW_TILE), cols] = acc
        return carry

    lax.fori_loop(0, t // ROW_TILE, tile, 0)


def _conv_rows_block(z_ref, w_ref, o_ref, zp_ref, cols):
    t = z_ref.shape[1]
    pad = CONV_PAD * ROW_TILE
    zp_ref[0:pad, :] = jnp.zeros((pad, LANES), F32)
    zp_ref[pad + t:pad + t + pad, :] = jnp.zeros((pad, LANES), F32)
    zp_ref[pad:pad + t, :] = z_ref[0, :, cols]
    wrows = [w_ref[j:j + 1, cols] for j in range(CONV_WIDTH)]

    def tile(i, carry):
        base = pl.multiple_of(i * ROW_TILE, ROW_TILE)
        acc = wrows[0] * zp_ref[pl.ds(base, ROW_TILE), :]
        for j in range(1, CONV_WIDTH):
            acc = acc + wrows[j] * zp_ref[pl.ds(base + j * ROW_TILE, ROW_TILE), :]
        o_ref[0, pl.ds(base, ROW_TILE), cols] = acc
        return carry

    lax.fori_loop(0, t // ROW_TILE, tile, 0)


def _conv_kernel(z_ref, w_ref, o_ref, zp_ref, *, plan):
    for blk, seg in enumerate(plan):
        cols = slice(blk * LANES, (blk + 1) * LANES)
        if seg is None:
            _conv_rows_block(z_ref, w_ref, o_ref, zp_ref, cols)
        else:
            _conv_seg_block(z_ref, w_ref, o_ref, zp_ref, cols, seg)


def _conv_call(z, w, plan):
    b, t, ch = z.shape
    assert ROW_TILE == GRID_W and len(plan) * LANES == ch
    rows = max((t + 2 * CONV_PAD * ROW_TILE) if seg is None else (t // seg) * (seg + 2 * SEG_LPAD)
               for seg in plan)
    return pl.pallas_call(
        functools.partial(_conv_kernel, plan=plan),
        grid=(b,),
        in_specs=[pl.BlockSpec((1, t, ch), lambda i: (i, 0, 0)),
                  pl.BlockSpec((CONV_WIDTH, ch), lambda i: (0, 0))],
        out_specs=pl.BlockSpec((1, t, ch), lambda i: (i, 0, 0)),
        out_shape=jax.ShapeDtypeStruct((b, t, ch), F32),
        scratch_shapes=[pltpu.VMEM((rows, LANES), F32)],
        compiler_params=_cparams(1),
        name="conv",
    )(z, w)


def _gate_rows(g, chunk):
    b, _, t = g.shape
    return g.reshape(b, 4, N_HEADS, t // chunk, chunk).transpose(0, 2, 1, 3, 4)


def kernel(x, c, ctx, c_ctx, w_mod, b_mod, norm_ffn1, ffn1_up, ffn1_down, norm_mix, w_in, b_in,
           mlstm_norm, conv_w, conv_b, conv_norm_g, conv_norm_b, w_out, norm_ffn2, ffn2_up,
           ffn2_down, norm_final):
    b, s, d = x.shape
    depth = w_mod.shape[0]
    d_ff = ffn1_down.shape[1]
    nck = d_ff // FF_CHUNK
    tm_c = ctx.shape[1]
    L = MLSTM_CHUNK

    def flat(a):
        return a.reshape(1, -1, a.shape[-1])

    rows = -(-(b + 1) // 8) * 8
    c_all = jnp.concatenate([c, c_ctx[None, :], jnp.zeros((rows - b - 1, d), F32)], axis=0)
    mods_all = _mod_call(c_all, w_mod, b_mod).reshape(depth, rows, N_MOD, 1, d)

    q0, k0, v0, o0 = 0, D_MLSTM, 2 * D_MLSTM, 3 * D_MLSTM
    g0 = 4 * D_MLSTM
    u0 = g0 + 4 * N_HEADS

    xc = ctx
    for l in range(depth):
        last = l == depth - 1
        mods = mods_all[l]

        def ffn_weights(up, down):
            return up.astype(BF16), down.reshape(nck, FF_CHUNK, d).astype(BF16)

        w1 = ffn_weights(ffn1_up[l], ffn1_down[l])
        x = _ffn_call(x, mods, norm_ffn1[l], *w1, js=0, mod_row=None, tm=FFN_TILE)
        xc = _ffn_call(flat(xc), mods, norm_ffn1[l], *w1, js=0, mod_row=b, tm=FFN_TILE).reshape(xc.shape)

        wi, bi = w_in[l], b_in[l]
        wn = jnp.concatenate([wi[:, k0:v0], wi[:, u0:]], axis=1).astype(BF16)
        bn = jnp.concatenate([bi[k0:v0], bi[u0:]]).reshape(1, -1)
        wt = jnp.concatenate([wi[:, q0:k0], wi[:, v0:u0]], axis=1).T.astype(BF16)
        bt = jnp.broadcast_to(jnp.concatenate([bi[q0:k0], bi[v0:u0]])[:, None],
                              (wt.shape[0], LANES))
        gain = jnp.broadcast_to(mlstm_norm[l][:, None], (D_MLSTM, LANES))
        kl, zl, gl, qtl, vtl, otl = _proj_call(x, mods, norm_mix[l], wn, bn, wt, bt, gain,
                                               mod_row=None, tm=PROJ_TILE, chunk=L)
        kc, zc, gc, qtc, vtc, otc = _proj_call(xc, mods, norm_mix[l], wn, bn, wt, bt, gain,
                                               mod_row=b, tm=tm_c, chunk=L)

        gates = jnp.concatenate([_gate_rows(gc, L), _gate_rows(gl, L)], axis=3)
        ymc, yml, z_cols, z_rows = _mlstm_call(kc, qtc, vtc, otc, kl, qtl, vtl, otl, gates,
                                               zl, conv_w[l], ctx_out=not last)
        nblk = D_CONV // LANES
        wm = w_out[l][:D_MLSTM].astype(BF16)
        wc = w_out[l][D_MLSTM:].astype(BF16)
        mix = (conv_b[l], conv_norm_g[l], conv_norm_b[l], wm, wc)

        w2 = ffn_weights(ffn2_up[l], ffn2_down[l])
        x = _ffn_call(x, mods, norm_ffn2[l], *w2, js=6, mod_row=None, tm=FFN_TILE,
                      final_g=norm_final if last else None, mix=(yml, [z_cols, z_rows], *mix))
        if not last:
            z_ctx = _conv_call(zc, conv_w[l], (zc.shape[1],) * nblk)
            xc = _ffn_call(flat(xc), mods, norm_ffn2[l], *w2, js=6, mod_row=b, tm=FFN_TILE,
                           mix=(flat(ymc), [flat(z_ctx)], *mix)).reshape(xc.shape)
    return x
```

```python
import functools

import jax
import jax.numpy as jnp
from jax import lax
from jax.experimental import pallas as pl
from jax.experimental.pallas import tpu as pltpu

F32 = jnp.float32
BF16 = jnp.bfloat16

GRID_W = 64
N_HEADS = 4
HEAD_DIM = 128
D_MLSTM = N_HEADS * HEAD_DIM
D_CONV = 512
CONV_WIDTH = 31
CONV_PAD = (CONV_WIDTH - 1) // 2
N_MOD = 9
EPS = 1e-6

LANES = 128
FF_CHUNK = 256
FFN_TILE = 1024
PROJ_TILE = 1024
FFN_UNROLL = 10
MLSTM_CHUNK = 128
STATE_ROWS = HEAD_DIM + 16
CONV_STAGES_PER_CHUNK = 3
SEG_LPAD = 16
ROW_TILE = 64
NEG_BIG = -1e30
LOG2E = 1.4426950408889634
VMEM_LIMIT = 56 * 1024 * 1024


def _cparams(n_axes):
    return pltpu.CompilerParams(
        dimension_semantics=("arbitrary",) * n_axes, vmem_limit_bytes=VMEM_LIMIT)


def _resident(shape):
    nd = len(shape)
    return pl.BlockSpec(shape, lambda *_: (0,) * nd, pipeline_mode=pl.Buffered(1))


def _rms_mod(x, g, shift, scale):
    ms = jnp.mean(x * x, axis=-1, keepdims=True)
    return (x * lax.rsqrt(ms + EPS)) * (g * (1.0 + scale)) + shift


def _mod_kernel(c_ref, w_ref, b_ref, o_ref):
    c = c_ref[...]
    s = (c * jax.nn.sigmoid(c)).astype(BF16)
    o_ref[0] = jnp.dot(s, w_ref[0].astype(BF16), preferred_element_type=F32) + b_ref[0]


def _mod_call(c_all, w_mod, b_mod):
    depth, d, n = w_mod.shape
    r = c_all.shape[0]
    tn = n // 8
    return pl.pallas_call(
        _mod_kernel,
        grid=(depth, n // tn),
        in_specs=[pl.BlockSpec((r, d), lambda l, j: (0, 0)),
                  pl.BlockSpec((1, d, tn), lambda l, j: (l, 0, j)),
                  pl.BlockSpec((1, 1, tn), lambda l, j: (l, 0, j))],
        out_specs=pl.BlockSpec((1, r, tn), lambda l, j: (l, 0, j)),
        out_shape=jax.ShapeDtypeStruct((depth, r, n), F32),
        compiler_params=_cparams(2),
        name="mod",
    )(c_all, w_mod, b_mod.reshape(depth, 1, n))


def _ffn_kernel(*refs, js, final, nz):
    x_ref, mod_ref = refs[:2]
    refs = refs[2:]
    if nz:
        ym_ref, z_refs = refs[0], refs[1:1 + nz]
        cb_ref, cg_ref, cnb_ref, wm_ref, wc_ref = refs[1 + nz:6 + nz]
        refs = refs[6 + nz:]
    ng_ref, wup_ref, wd_ref = refs[:3]
    refs = refs[3:]
    if final:
        gf_ref, refs = refs[0], refs[1:]
    o_ref, h_ref, a_ref, acc_ref = refs
    nck, fc, _ = wd_ref.shape
    d_ff = nck * fc

    x = x_ref[0]
    if nz:
        z = jnp.concatenate([r[0] for r in z_refs], axis=-1) if nz > 1 else z_refs[0][0]
        z = z + cb_ref[...]
        zc = z - jnp.mean(z, axis=-1, keepdims=True)
        var = jnp.mean(zc * zc, axis=-1, keepdims=True)
        ln = zc * lax.rsqrt(var + EPS) * cg_ref[...] + cnb_ref[...]
        yc = (ln * jax.nn.sigmoid(ln)).astype(BF16)
        y = (jnp.dot(ym_ref[0], wm_ref[...], preferred_element_type=F32)
             + jnp.dot(yc, wc_ref[...], preferred_element_type=F32))
        x = x + mod_ref[0, 5] * y
    o_ref[0] = x
    h_ref[...] = _rms_mod(x, ng_ref[...], mod_ref[0, js], mod_ref[0, js + 1]).astype(BF16)

    def up(c):
        h = h_ref[...]
        col = c * fc if isinstance(c, int) else pl.multiple_of(c * fc, fc)
        g = jnp.dot(h, wup_ref[:, pl.ds(col, fc)], preferred_element_type=F32)
        u = jnp.dot(h, wup_ref[:, pl.ds(d_ff + col, fc)], preferred_element_type=F32)
        return (g * jax.nn.sigmoid(g) * u).astype(BF16)

    a_ref[0] = up(0)
    acc_ref[...] = jnp.zeros_like(acc_ref)

    def chunk(c, carry):
        a_prev = a_ref[(c - 1) % 2]
        a_ref[c % 2] = up(c)
        acc_ref[...] += jnp.dot(a_prev, wd_ref[c - 1], preferred_element_type=F32)
        return carry

    lax.fori_loop(1, nck, chunk, 0, unroll=FFN_UNROLL)
    d_last = jnp.dot(a_ref[(nck - 1) % 2], wd_ref[nck - 1], preferred_element_type=F32)
    out = o_ref[0] + (0.5 * mod_ref[0, js + 2]) * (acc_ref[...] + d_last)
    if final:
        ms = jnp.mean(out * out, axis=-1, keepdims=True)
        out = out * lax.rsqrt(ms + EPS) * gf_ref[...]
    o_ref[0] = out


def _ffn_call(x, mods, ng, wup, wd, *, js, mod_row, tm, final_g=None, mix=None):
    b, t, d = x.shape
    fc = wd.shape[1]
    row = (lambda i, j: (i, 0, 0, 0)) if mod_row is None else (lambda i, j: (mod_row, 0, 0, 0))

    def tok(c):
        return pl.BlockSpec((1, tm, c), lambda i, j: (i, j, 0))

    in_specs = [tok(d), pl.BlockSpec((1, N_MOD, 1, d), row)]
    args = [x, mods]
    nz = 0
    if mix is not None:
        ym, zs, cb, cg, cnb, wm, wc = mix
        nz = len(zs)
        in_specs += [tok(ym.shape[2])] + [tok(z.shape[2]) for z in zs]
        in_specs += [_resident((1, D_CONV))] * 3 + [_resident(wm.shape), _resident(wc.shape)]
        args += [ym, *zs, cb.reshape(1, -1), cg.reshape(1, -1), cnb.reshape(1, -1), wm, wc]
    in_specs += [_resident((1, d)), _resident(wup.shape), _resident(wd.shape)]
    args += [ng.reshape(1, d), wup, wd]
    if final_g is not None:
        in_specs.append(_resident((1, d)))
        args.append(final_g.reshape(1, d))
    return pl.pallas_call(
        functools.partial(_ffn_kernel, js=js, final=final_g is not None, nz=nz),
        grid=(b, t // tm),
        in_specs=in_specs,
        out_specs=tok(d),
        out_shape=jax.ShapeDtypeStruct((b, t, d), F32),
        scratch_shapes=[pltpu.VMEM((tm, d), BF16), pltpu.VMEM((2, tm, fc), BF16),
                        pltpu.VMEM((tm, d), F32)],
        compiler_params=_cparams(2),
        name="ffn",
    )(*args)


def _proj_kernel(x_ref, mod_ref, ng_ref, wn_ref, bn_ref, wt_ref, bt_ref, gain_ref,
                 k_ref, z_ref, g_ref, qt_ref, vt_ref, ot_ref, *, chunk):
    tm = x_ref.shape[1]
    u = _rms_mod(x_ref[0], ng_ref[...], mod_ref[0, 3], mod_ref[0, 4]).astype(BF16)
    pn = jnp.dot(u, wn_ref[...], preferred_element_type=F32) + bn_ref[...]
    for h in range(N_HEADS):
        k_ref[0, h] = pn[:, h * HEAD_DIM:(h + 1) * HEAD_DIM].astype(BF16)
    a = pn[:, D_MLSTM:D_MLSTM + D_CONV]
    gl = pn[:, D_MLSTM + D_CONV:D_MLSTM + 2 * D_CONV]
    z_ref[0] = a * jax.nn.sigmoid(gl)
    pt = lax.dot_general(wt_ref[...], u, (((1,), (1,)), ((), ())), preferred_element_type=F32)
    rg = slice(3 * D_MLSTM, 3 * D_MLSTM + 4 * N_HEADS)
    for lb in range(tm // LANES):
        cols = slice(lb * LANES, (lb + 1) * LANES)
        g_ref[0, :, cols] = pt[rg, cols] + bt_ref[rg, :]
    scale = HEAD_DIM ** -0.5
    for h in range(N_HEADS):
        for lb in range(tm // LANES):
            c, off = divmod(lb * LANES, chunk)
            cols = slice(lb * LANES, (lb + 1) * LANES)
            dst = (0, h, c, slice(None), slice(off, off + LANES))
            rq = slice(h * HEAD_DIM, (h + 1) * HEAD_DIM)
            rv = slice(D_MLSTM + h * HEAD_DIM, D_MLSTM + (h + 1) * HEAD_DIM)
            ro = slice(2 * D_MLSTM + h * HEAD_DIM, 2 * D_MLSTM + (h + 1) * HEAD_DIM)
            qt_ref[dst] = ((pt[rq, cols] + bt_ref[rq, :]) * scale).astype(BF16)
            vt_ref[dst] = (pt[rv, cols] + bt_ref[rv, :]).astype(BF16)
            ot_ref[dst] = jax.nn.sigmoid(pt[ro, cols] + bt_ref[ro, :]) * gain_ref[rq, :]


def _proj_call(x, mods, ng, wn, bn, wt, bt, gain, *, mod_row, tm, chunk):
    b, t, d = x.shape
    nch, cpt = t // chunk, tm // chunk
    row = (lambda i, j: (i, 0, 0, 0)) if mod_row is None else (lambda i, j: (mod_row, 0, 0, 0))
    tspec = pl.BlockSpec((1, N_HEADS, cpt, HEAD_DIM, chunk), lambda i, j: (i, 0, j, 0, 0))
    tshape = (b, N_HEADS, nch, HEAD_DIM, chunk)
    return pl.pallas_call(
        functools.partial(_proj_kernel, chunk=chunk),
        grid=(b, t // tm),
        in_specs=[pl.BlockSpec((1, tm, d), lambda i, j: (i, j, 0)),
                  pl.BlockSpec((1, N_MOD, 1, d), row),
                  _resident((1, d)), _resident(wn.shape), _resident(bn.shape),
                  _resident(wt.shape), _resident(bt.shape), _resident(gain.shape)],
        out_specs=[pl.BlockSpec((1, N_HEADS, tm, HEAD_DIM), lambda i, j: (i, 0, j, 0)),
                   pl.BlockSpec((1, tm, D_CONV), lambda i, j: (i, j, 0)),
                   pl.BlockSpec((1, 4 * N_HEADS, tm), lambda i, j: (i, 0, j)),
                   tspec, tspec, tspec],
        out_shape=[jax.ShapeDtypeStruct((b, N_HEADS, t, HEAD_DIM), BF16),
                   jax.ShapeDtypeStruct((b, t, D_CONV), F32),
                   jax.ShapeDtypeStruct((b, 4 * N_HEADS, t), F32),
                   jax.ShapeDtypeStruct(tshape, BF16),
                   jax.ShapeDtypeStruct(tshape, BF16),
                   jax.ShapeDtypeStruct(tshape, F32)],
        compiler_params=_cparams(2),
        name="proj",
    )(x, mods, ng.reshape(1, d), wn, bn, wt, bt, gain)


def _mlstm_state_pass(g_ref, kc_ref, vtc_ref, kl_ref, vtl_ref, hh, scratch, masks, orders):
    gs_ref, rt_ref, upd_ref, st_ref = scratch
    _, n_ch, L = g_ref.shape[2:]
    for d in range(2):
        li, fpre = g_ref[0, hh, 2 * d], g_ref[0, hh, 2 * d + 1]
        lf = jnp.minimum(fpre, 0.0) - jnp.log1p(jnp.exp(-jnp.abs(fpre)))
        tri = masks[d].astype(F32)
        cum = jnp.dot(lf, tri, preferred_element_type=F32, precision=lax.Precision.HIGHEST)
        edge = L - 1 if d == 0 else 0
        total = cum[:, edge:edge + 1]
        r = li - cum
        yield
        pm = r
        lane = lax.broadcasted_iota(jnp.int32, r.shape, 1)
        s = 1
        while s < L:
            if d == 1:
                sh = jnp.where(lane < L - s, pltpu.roll(pm, L - s, 1), NEG_BIG)
            else:
                sh = jnp.where(lane >= s, pltpu.roll(pm, s, 1), NEG_BIG)
            pm = jnp.maximum(pm, sh)
            s *= 2
            yield
        rmax = pm[:, edge:edge + 1]
        m = jnp.zeros((1, 1), F32)
        m_prev, m_new = [None] * n_ch, [None] * n_ch
        for c in orders[d]:
            m_prev[c] = m
            m = total[c:c + 1] + jnp.maximum(m, rmax[c:c + 1])
            m_new[c] = m
        yield
        m_prev = jnp.concatenate(m_prev, axis=0)
        m_new = jnp.concatenate(m_new, axis=0)
        mx = jnp.maximum(m_prev, pm)
        alpha = -LOG2E * mx
        inter = jnp.exp(m_prev - mx)
        clampv = jnp.exp(-(cum + mx))
        wk = jnp.exp(total + r - m_new)
        decay = jnp.broadcast_to(jnp.exp(total + m_prev - m_new), (n_ch, L))
        rt_ref[d] = jnp.concatenate([LOG2E * r, jnp.zeros((LANES - n_ch, L), F32)], axis=0).T
        pad = jnp.zeros((3, L), F32)
        for c in range(n_ch):
            gs_ref[d, c] = jnp.concatenate(
                [alpha[c:c + 1], inter[c:c + 1], clampv[c:c + 1], wk[c:c + 1], decay[c:c + 1], pad],
                axis=0)
        yield

    n_c = kc_ref.shape[2] // L
    for c in range(n_ch):
        if c < n_c:
            k, vt = kc_ref[0, hh, c * L:(c + 1) * L, :], vtc_ref[0, hh, c]
        else:
            k, vt = kl_ref[0, hh, (c - n_c) * L:(c - n_c + 1) * L, :], vtl_ref[0, hh, c - n_c]
        vtf = vt.astype(F32)
        parts = []
        for d in range(2):
            wk = gs_ref[d, c][3:4]
            parts += [vtf * wk, wk, jnp.zeros((STATE_ROWS - HEAD_DIM - 1, L), F32)]
        lhs = jnp.concatenate(parts, axis=0).astype(BF16)
        upd = jnp.dot(lhs, k, preferred_element_type=F32)
        upd_ref[0, c] = upd[:STATE_ROWS]
        upd_ref[1, c] = upd[STATE_ROWS:]
        yield

    states = [jnp.zeros((STATE_ROWS, HEAD_DIM), F32)] * 2
    for i in range(n_ch):
        for d in range(2):
            c = orders[d][i]
            st_ref[d, c] = states[d].astype(BF16)
            states[d] = gs_ref[d, c][4:5, :HEAD_DIM] * states[d] + upd_ref[d, c]
        yield


def _conv_cols_pass(z_ref, w_ref, o_ref, zp_ref):
    seg, pitch = GRID_W, GRID_W + 2 * SEG_LPAD
    n_seg = z_ref.shape[1] // seg
    zp_ref[...] = jnp.zeros_like(zp_ref)
    yield
    for s in range(n_seg):
        zp_ref[s * pitch + SEG_LPAD:s * pitch + SEG_LPAD + seg, :] = z_ref[0, s * seg:(s + 1) * seg, :]
        if s % 8 == 7:
            yield
    for s in range(n_seg):
        src = s * pitch + SEG_LPAD - CONV_PAD
        acc = w_ref[0:1, :] * zp_ref[src:src + seg, :]
        for j in range(1, CONV_WIDTH):
            acc = acc + w_ref[j:j + 1, :] * zp_ref[src + j:src + j + seg, :]
        o_ref[0, s * seg:(s + 1) * seg, :] = acc
        yield


def _conv_rows_pass(z_ref, w_ref, o_ref):
    n = z_ref.shape[1] // GRID_W
    for i in range(n):
        acc = None
        for j in range(CONV_WIDTH):
            src = i + j - CONV_PAD
            if 0 <= src < n:
                term = w_ref[j:j + 1, :] * z_ref[0, src * GRID_W:(src + 1) * GRID_W, :]
                acc = term if acc is None else acc + term
        o_ref[0, i * GRID_W:(i + 1) * GRID_W, :] = acc
        yield


def _mlstm_kernel(kc_ref, qtc_ref, vtc_ref, otc_ref, kl_ref, qtl_ref, vtl_ref, otl_ref, g_ref,
                  kcn_ref, vtcn_ref, kln_ref, vtln_ref, gn_ref, zcol_ref, zrow_ref, wcol_ref, wrow_ref,
                  yc_ref, yl_ref, ocol_ref, orow_ref, *scratch, ctx_out):
    n_c, n_l = qtc_ref.shape[2], qtl_ref.shape[2]
    n_ch = n_c + n_l
    L = qtl_ref.shape[4]
    srow = lax.broadcasted_iota(jnp.int32, (L, L), 0)
    tcol = lax.broadcasted_iota(jnp.int32, (L, L), 1)
    masks = (srow <= tcol, srow >= tcol)
    orders = (list(range(n_ch)),
              list(range(n_c - 1, -1, -1)) + list(range(n_ch - 1, n_c - 1, -1)))
    sets = (scratch[:4], scratch[4:8])
    zp_ref = scratch[8]
    cur_refs = (g_ref, kc_ref, vtc_ref, kl_ref, vtl_ref)
    next_refs = (gn_ref, kcn_ref, vtcn_ref, kln_ref, vtln_ref)

    @pl.when((pl.program_id(0) == 0) & (pl.program_id(1) == 0))
    def _():
        for _ in _mlstm_state_pass(*cur_refs, 0, sets[0], masks, orders):
            pass

    for hh in range(2):
        if hh == 0:
            ahead = _mlstm_state_pass(*cur_refs, 1, sets[1], masks, orders)
            conv = _conv_cols_pass(zcol_ref, wcol_ref, ocol_ref, zp_ref)
        else:
            ahead = _mlstm_state_pass(*next_refs, 0, sets[0], masks, orders)
            conv = _conv_rows_pass(zrow_ref, wrow_ref, orow_ref)
        _mlstm_outputs(kc_ref, qtc_ref, vtc_ref, otc_ref, kl_ref, qtl_ref, vtl_ref, otl_ref,
                       yc_ref, yl_ref, hh, sets[hh], masks, (ahead, conv), ctx_out)
        for gen in (ahead, conv):
            for _ in gen:
                pass


def _mlstm_outputs(kc_ref, qtc_ref, vtc_ref, otc_ref, kl_ref, qtl_ref, vtl_ref, otl_ref,
                   yc_ref, yl_ref, hh, scratch, masks, fillers, ctx_out):
    gs_ref, rt_ref, _, st_ref = scratch
    n_c, n_l = qtc_ref.shape[2], qtl_ref.shape[2]
    L = qtl_ref.shape[4]
    lanes = slice(hh * HEAD_DIM, (hh + 1) * HEAD_DIM)
    ahead, conv = fillers

    def tick(boundary=False):
        next(ahead, None)
        if boundary:
            for _ in range(CONV_STAGES_PER_CHUNK):
                next(conv, None)

    def outputs(cidx, k, qt, vt, og):
        s_t = jnp.dot(k, qt, preferred_element_type=F32)
        carried = jnp.dot(jnp.concatenate([st_ref[0, cidx], st_ref[1, cidx]], axis=0), qt,
                          preferred_element_type=F32)
        tick()
        sws, dens, rows = [], [], []
        for d in range(2):
            gsl = gs_ref[d, cidx]
            rows.append(gsl)
            arg = jnp.where(masks[d], rt_ref[d, :, cidx:cidx + 1] + gsl[0:1], NEG_BIG)
            sw = s_t * jnp.exp2(arg)
            sws.append(sw.astype(BF16))
            qn = carried[d * STATE_ROWS + HEAD_DIM:d * STATE_ROWS + HEAD_DIM + 1]
            dens.append(jnp.sum(sw, axis=0, keepdims=True) + gsl[1:2] * qn)
            tick()
        num2 = jnp.dot(vt, jnp.concatenate(sws, axis=1), preferred_element_type=F32)
        tick()
        hsum = None
        for d in range(2):
            num = (num2[:, d * L:(d + 1) * L]
                   + rows[d][1:2] * carried[d * STATE_ROWS:d * STATE_ROWS + HEAD_DIM])
            h = num * (1.0 / jnp.maximum(jnp.abs(dens[d]), rows[d][2:3]))
            hsum = h if hsum is None else hsum + h
        ms = jnp.mean(hsum * hsum, axis=0, keepdims=True)
        y = hsum * lax.rsqrt(ms + EPS) * og
        return y.T.astype(BF16)

    if ctx_out:
        for c in range(n_c):
            yc_ref[0, c * L:(c + 1) * L, lanes] = outputs(
                c, kc_ref[0, hh, c * L:(c + 1) * L, :], qtc_ref[0, hh, c], vtc_ref[0, hh, c],
                otc_ref[0, hh, c])
            tick(boundary=True)
    else:
        yc_ref[0, :, lanes] = jnp.zeros((yc_ref.shape[1], HEAD_DIM), yc_ref.dtype)

    for c in range(n_l):
        yl_ref[0, c * L:(c + 1) * L, lanes] = outputs(
            n_c + c, kl_ref[0, hh, c * L:(c + 1) * L, :], qtl_ref[0, hh, c], vtl_ref[0, hh, c],
            otl_ref[0, hh, c])
        tick(boundary=True)


def _mlstm_call(kc, qtc, vtc, otc, kl, qtl, vtl, otl, gates, z, w_conv, *, ctx_out):
    b = kl.shape[0]
    tc, tl = kc.shape[2], kl.shape[2]
    n_c, n_l, L = qtc.shape[2], qtl.shape[2], qtl.shape[4]
    n_ch = n_c + n_l
    assert L == LANES and n_ch <= L

    pairs = N_HEADS // 2

    def nat(t):
        return pl.BlockSpec((1, 2, t, HEAD_DIM), lambda i, j: (i, j, 0, 0))

    def tr(n):
        return pl.BlockSpec((1, 2, n, HEAD_DIM, L), lambda i, j: (i, j, 0, 0, 0))

    def nxt(*tail):
        def index(i, j):
            f = jnp.minimum(i * pairs + j + 1, b * pairs - 1)
            return (f // pairs, 2 * (f % pairs)) + tail
        return index

    def nat_next(t):
        return pl.BlockSpec((1, 1, t, HEAD_DIM), nxt(0, 0))

    def tr_next(n):
        return pl.BlockSpec((1, 1, n, HEAD_DIM, L), nxt(0, 0, 0))

    scratch_set = [pltpu.VMEM((2, n_ch, 8, L), F32),
                   pltpu.VMEM((2, L, LANES), F32),
                   pltpu.VMEM((2, n_ch, STATE_ROWS, HEAD_DIM), F32),
                   pltpu.VMEM((2, n_ch, STATE_ROWS, HEAD_DIM), BF16)]
    assert D_CONV == 2 * pairs * LANES and tl % GRID_W == 0
    zblk = (1, tl, LANES)
    half = D_CONV // 2
    return pl.pallas_call(
        functools.partial(_mlstm_kernel, ctx_out=ctx_out),
        grid=(b, pairs),
        in_specs=[nat(tc), tr(n_c), tr(n_c), tr(n_c), nat(tl), tr(n_l), tr(n_l), tr(n_l),
                  pl.BlockSpec((1, 2, 4, n_ch, L), lambda i, j: (i, j, 0, 0, 0)),
                  nat_next(tc), tr_next(n_c), nat_next(tl), tr_next(n_l),
                  pl.BlockSpec((1, 1, 4, n_ch, L), nxt(0, 0, 0)),
                  pl.BlockSpec(zblk, lambda i, j: (i, 0, j)),
                  pl.BlockSpec(zblk, lambda i, j: (i, 0, pairs + j)),
                  pl.BlockSpec((CONV_WIDTH, LANES), lambda i, j: (0, j)),
                  pl.BlockSpec((CONV_WIDTH, LANES), lambda i, j: (0, pairs + j))],
        out_specs=[pl.BlockSpec((1, tc, 2 * HEAD_DIM), lambda i, j: (i, 0, j)),
                   pl.BlockSpec((1, tl, 2 * HEAD_DIM), lambda i, j: (i, 0, j)),
                   pl.BlockSpec(zblk, lambda i, j: (i, 0, j)),
                   pl.BlockSpec(zblk, lambda i, j: (i, 0, j))],
        out_shape=[jax.ShapeDtypeStruct((b, tc, D_MLSTM), BF16),
                   jax.ShapeDtypeStruct((b, tl, D_MLSTM), BF16),
                   jax.ShapeDtypeStruct((b, tl, half), F32),
                   jax.ShapeDtypeStruct((b, tl, half), F32)],
        scratch_shapes=scratch_set + scratch_set
                       + [pltpu.VMEM(((tl // GRID_W) * (GRID_W + 2 * SEG_LPAD), LANES), F32)],
        compiler_params=_cparams(2),
        name="mlstm",
    )(kc, qtc, vtc, otc, kl, qtl, vtl, otl, gates, kc, vtc, kl, vtl, gates, z, z, w_conv, w_conv)


def _conv_kernel(z_ref, w_ref, o_ref, zp_ref, *, seg):
    t, ch = z_ref.shape[1], z_ref.shape[2]
    pitch = seg + 2 * SEG_LPAD
    n_seg = t // seg
    tiles = seg // ROW_TILE
    for blk in range(ch // LANES):
        cols = slice(blk * LANES, (blk + 1) * LANES)
        zp_ref[...] = jnp.zeros_like(zp_ref)
        for s in range(n_seg):
            zp_ref[s * pitch + SEG_LPAD:s * pitch + SEG_LPAD + seg, :] = z_ref[0, s * seg:(s + 1) * seg, cols]
        wrows = [w_ref[j:j + 1, cols] for j in range(CONV_WIDTH)]

        def tile(i, carry):
            s, q = i // tiles, i % tiles
            src = s * pitch + q * ROW_TILE + (SEG_LPAD - CONV_PAD)
            acc = wrows[0] * zp_ref[pl.ds(src, ROW_TILE), :]
            for j in range(1, CONV_WIDTH):
                acc = acc + wrows[j] * zp_ref[pl.ds(src + j, ROW_TILE), :]
            o_ref[0, pl.ds(pl.multiple_of(i * ROW_TILE, ROW_TILE), ROW_TILE), cols] = acc
            return carry

        lax.fori_loop(0, t // ROW_TILE, tile, 0)


def _conv_call(z, w, seg):
    b, t, ch = z.shape
    assert t % seg == 0 and seg % ROW_TILE == 0 and ch % LANES == 0
    return pl.pallas_call(
        functools.partial(_conv_kernel, seg=seg),
        grid=(b,),
        in_specs=[pl.BlockSpec((1, t, ch), lambda i: (i, 0, 0)),
                  pl.BlockSpec((CONV_WIDTH, ch), lambda i: (0, 0))],
        out_specs=pl.BlockSpec((1, t, ch), lambda i: (i, 0, 0)),
        out_shape=jax.ShapeDtypeStruct((b, t, ch), F32),
        scratch_shapes=[pltpu.VMEM(((t // seg) * (seg + 2 * SEG_LPAD), LANES), F32)],
        compiler_params=_cparams(1),
        name="conv",
    )(z, w)


def _gate_rows(g, chunk):
    b, _, t = g.shape
    return g.reshape(b, 4, N_HEADS, t // chunk, chunk).transpose(0, 2, 1, 3, 4)


def kernel(x, c, ctx, c_ctx, w_mod, b_mod, norm_ffn1, ffn1_up, ffn1_down, norm_mix, w_in, b_in,
           mlstm_norm, conv_w, conv_b, conv_norm_g, conv_norm_b, w_out, norm_ffn2, ffn2_up,
           ffn2_down, norm_final):
    b, s, d = x.shape
    depth = w_mod.shape[0]
    d_ff = ffn1_down.shape[1]
    nck = d_ff // FF_CHUNK
    tm_c = ctx.shape[1]
    L = MLSTM_CHUNK

    def flat(a):
        return a.reshape(1, -1, a.shape[-1])

    rows = -(-(b + 1) // 8) * 8
    c_all = jnp.concatenate([c, c_ctx[None, :], jnp.zeros((rows - b - 1, d), F32)], axis=0)
    mods_all = _mod_call(c_all, w_mod, b_mod).reshape(depth, rows, N_MOD, 1, d)

    q0, k0, v0 = 0, D_MLSTM, 2 * D_MLSTM
    u0 = 4 * D_MLSTM + 4 * N_HEADS

    xc = ctx
    for l in range(depth):
        last = l == depth - 1
        mods = mods_all[l]

        def ffn_weights(up, down):
            return up.astype(BF16), down.reshape(nck, FF_CHUNK, d).astype(BF16)

        w1 = ffn_weights(ffn1_up[l], ffn1_down[l])
        x = _ffn_call(x, mods, norm_ffn1[l], *w1, js=0, mod_row=None, tm=FFN_TILE)
        xc = _ffn_call(flat(xc), mods, norm_ffn1[l], *w1, js=0, mod_row=b, tm=FFN_TILE).reshape(xc.shape)

        wi, bi = w_in[l], b_in[l]
        wn = jnp.concatenate([wi[:, k0:v0], wi[:, u0:]], axis=1).astype(BF16)
        bn = jnp.concatenate([bi[k0:v0], bi[u0:]]).reshape(1, -1)
        wt = jnp.concatenate([wi[:, q0:k0], wi[:, v0:u0]], axis=1).T.astype(BF16)
        bt = jnp.broadcast_to(jnp.concatenate([bi[q0:k0], bi[v0:u0]])[:, None],
                              (wt.shape[0], LANES))
        gain = jnp.broadcast_to(mlstm_norm[l][:, None], (D_MLSTM, LANES))
        kl, zl, gl, qtl, vtl, otl = _proj_call(x, mods, norm_mix[l], wn, bn, wt, bt, gain,
                                               mod_row=None, tm=PROJ_TILE, chunk=L)
        kc, zc, gc, qtc, vtc, otc = _proj_call(xc, mods, norm_mix[l], wn, bn, wt, bt, gain,
                                               mod_row=b, tm=tm_c, chunk=L)

        gates = jnp.concatenate([_gate_rows(gc, L), _gate_rows(gl, L)], axis=3)
        ymc, yml, z_cols, z_rows = _mlstm_call(kc, qtc, vtc, otc, kl, qtl, vtl, otl, gates,
                                               zl, conv_w[l], ctx_out=not last)
        wm = w_out[l][:D_MLSTM].astype(BF16)
        wc = w_out[l][D_MLSTM:].astype(BF16)
        mix = (conv_b[l], conv_norm_g[l], conv_norm_b[l], wm, wc)

        w2 = ffn_weights(ffn2_up[l], ffn2_down[l])
        x = _ffn_call(x, mods, norm_ffn2[l], *w2, js=6, mod_row=None, tm=FFN_TILE,
                      final_g=norm_final if last else None, mix=(yml, [z_cols, z_rows], *mix))
        if not last:
            z_ctx = _conv_call(zc, conv_w[l], zc.shape[1])
            xc = _ffn_call(flat(xc), mods, norm_ffn2[l], *w2, js=6, mod_row=b, tm=FFN_TILE,
                           mix=(flat(ymc), [flat(z_ctx)], *mix)).reshape(xc.shape)
    return x
```

```python
import functools

import jax
import jax.numpy as jnp
from jax import lax
from jax.experimental import pallas as pl
from jax.experimental.pallas import tpu as pltpu

F32 = jnp.float32
BF16 = jnp.bfloat16

GRID_W = 64
N_HEADS = 4
HEAD_DIM = 128
D_MLSTM = N_HEADS * HEAD_DIM
D_CONV = 512
CONV_WIDTH = 31
CONV_PAD = (CONV_WIDTH - 1) // 2
N_MOD = 9
EPS = 1e-6

LANES = 128
FF_CHUNK = 256
FFN_TILE = 1024
PROJ_TILE = 1024
PROLOGUE_SLICES = 4
FFN_UNROLL = 10
MLSTM_CHUNK = 128
STATE_ROWS = HEAD_DIM + 16
CONV_STAGES_PER_CHUNK = 3
SEG_LPAD = 16
ROW_TILE = 64
NEG_BIG = -1e30
LOG2E = 1.4426950408889634
VMEM_LIMIT = 56 * 1024 * 1024


def _cparams(n_axes):
    return pltpu.CompilerParams(
        dimension_semantics=("arbitrary",) * n_axes, vmem_limit_bytes=VMEM_LIMIT)


def _resident(shape):
    nd = len(shape)
    return pl.BlockSpec(shape, lambda *_: (0,) * nd, pipeline_mode=pl.Buffered(1))


def _rms_mod(x, g, shift, scale):
    ms = jnp.mean(x * x, axis=-1, keepdims=True)
    return (x * lax.rsqrt(ms + EPS)) * (g * (1.0 + scale)) + shift


def _mod_kernel(c_ref, w_ref, b_ref, o_ref):
    c = c_ref[...]
    s = (c * jax.nn.sigmoid(c)).astype(BF16)
    o_ref[0] = jnp.dot(s, w_ref[0].astype(BF16), preferred_element_type=F32) + b_ref[0]


def _mod_call(c_all, w_mod, b_mod):
    depth, d, n = w_mod.shape
    r = c_all.shape[0]
    tn = n // 8
    return pl.pallas_call(
        _mod_kernel,
        grid=(depth, n // tn),
        in_specs=[pl.BlockSpec((r, d), lambda l, j: (0, 0)),
                  pl.BlockSpec((1, d, tn), lambda l, j: (l, 0, j)),
                  pl.BlockSpec((1, 1, tn), lambda l, j: (l, 0, j))],
        out_specs=pl.BlockSpec((1, r, tn), lambda l, j: (l, 0, j)),
        out_shape=jax.ShapeDtypeStruct((depth, r, n), F32),
        compiler_params=_cparams(2),
        name="mod",
    )(c_all, w_mod, b_mod.reshape(depth, 1, n))


def _ffn_kernel(*refs, js, final, nz):
    x_ref, mod_ref = refs[:2]
    refs = refs[2:]
    if nz:
        ym_ref, z_refs = refs[0], refs[1:1 + nz]
        cb_ref, cg_ref, cnb_ref, wm_ref, wc_ref = refs[1 + nz:6 + nz]
        refs = refs[6 + nz:]
    ng_ref, wup_ref, wd_ref = refs[:3]
    refs = refs[3:]
    if final:
        gf_ref, refs = refs[0], refs[1:]
    o_ref, h_ref, a_ref, acc_ref = refs
    nck, fc, _ = wd_ref.shape
    d_ff = nck * fc

    def up(c, rows=slice(None)):
        h = h_ref[rows, :]
        col = c * fc if isinstance(c, int) else pl.multiple_of(c * fc, fc)
        g = jnp.dot(h, wup_ref[:, pl.ds(col, fc)], preferred_element_type=F32)
        u = jnp.dot(h, wup_ref[:, pl.ds(d_ff + col, fc)], preferred_element_type=F32)
        return (g * jax.nn.sigmoid(g) * u).astype(BF16)

    tm = x_ref.shape[1]
    n_slices = 1 if nz else PROLOGUE_SLICES
    for r in range(n_slices):
        rows = slice(r * (tm // n_slices), (r + 1) * (tm // n_slices))
        x = x_ref[0, rows, :]
        if nz:
            zs = [zr[0, rows, :] for zr in z_refs]
            z = (jnp.concatenate(zs, axis=-1) if nz > 1 else zs[0]) + cb_ref[...]
            zc = z - jnp.mean(z, axis=-1, keepdims=True)
            var = jnp.mean(zc * zc, axis=-1, keepdims=True)
            ln = zc * lax.rsqrt(var + EPS) * cg_ref[...] + cnb_ref[...]
            yc = (ln * jax.nn.sigmoid(ln)).astype(BF16)
            y = (jnp.dot(ym_ref[0, rows, :], wm_ref[...], preferred_element_type=F32)
                 + jnp.dot(yc, wc_ref[...], preferred_element_type=F32))
            x = x + mod_ref[0, 5] * y
        o_ref[0, rows, :] = x
        h_ref[rows, :] = _rms_mod(x, ng_ref[...], mod_ref[0, js], mod_ref[0, js + 1]).astype(BF16)
        a_ref[0, rows, :] = up(0, rows)

    acc_ref[...] = jnp.zeros_like(acc_ref)

    def chunk(c, carry):
        a_prev = a_ref[(c - 1) % 2]
        a_ref[c % 2] = up(c)
        acc_ref[...] += jnp.dot(a_prev, wd_ref[c - 1], preferred_element_type=F32)
        return carry

    lax.fori_loop(1, nck, chunk, 0, unroll=FFN_UNROLL)
    d_last = jnp.dot(a_ref[(nck - 1) % 2], wd_ref[nck - 1], preferred_element_type=F32)
    out = o_ref[0] + (0.5 * mod_ref[0, js + 2]) * (acc_ref[...] + d_last)
    if final:
        ms = jnp.mean(out * out, axis=-1, keepdims=True)
        out = out * lax.rsqrt(ms + EPS) * gf_ref[...]
    o_ref[0] = out


def _ffn_call(x, mods, ng, wup, wd, *, js, mod_row, tm, final_g=None, mix=None):
    b, t, d = x.shape
    fc = wd.shape[1]
    row = (lambda i, j: (i, 0, 0, 0)) if mod_row is None else (lambda i, j: (mod_row, 0, 0, 0))

    def tok(c):
        return pl.BlockSpec((1, tm, c), lambda i, j: (i, j, 0))

    in_specs = [tok(d), pl.BlockSpec((1, N_MOD, 1, d), row)]
    args = [x, mods]
    nz = 0
    if mix is not None:
        ym, zs, cb, cg, cnb, wm, wc = mix
        nz = len(zs)
        in_specs += [tok(ym.shape[2])] + [tok(z.shape[2]) for z in zs]
        in_specs += [_resident((1, D_CONV))] * 3 + [_resident(wm.shape), _resident(wc.shape)]
        args += [ym, *zs, cb.reshape(1, -1), cg.reshape(1, -1), cnb.reshape(1, -1), wm, wc]
    in_specs += [_resident((1, d)), _resident(wup.shape), _resident(wd.shape)]
    args += [ng.reshape(1, d), wup, wd]
    if final_g is not None:
        in_specs.append(_resident((1, d)))
        args.append(final_g.reshape(1, d))
    return pl.pallas_call(
        functools.partial(_ffn_kernel, js=js, final=final_g is not None, nz=nz),
        grid=(b, t // tm),
        in_specs=in_specs,
        out_specs=tok(d),
        out_shape=jax.ShapeDtypeStruct((b, t, d), F32),
        scratch_shapes=[pltpu.VMEM((tm, d), BF16), pltpu.VMEM((2, tm, fc), BF16),
                        pltpu.VMEM((tm, d), F32)],
        compiler_params=_cparams(2),
        name="ffn",
    )(*args)


def _proj_kernel(x_ref, mod_ref, ng_ref, wn_ref, bn_ref, wt_ref, bt_ref, gain_ref,
                 k_ref, z_ref, g_ref, qt_ref, vt_ref, ot_ref, *, chunk):
    tm = x_ref.shape[1]
    u = _rms_mod(x_ref[0], ng_ref[...], mod_ref[0, 3], mod_ref[0, 4]).astype(BF16)
    pn = jnp.dot(u, wn_ref[...], preferred_element_type=F32) + bn_ref[...]
    for h in range(N_HEADS):
        k_ref[0, h] = pn[:, h * HEAD_DIM:(h + 1) * HEAD_DIM].astype(BF16)
    a = pn[:, D_MLSTM:D_MLSTM + D_CONV]
    gl = pn[:, D_MLSTM + D_CONV:D_MLSTM + 2 * D_CONV]
    z_ref[0] = a * jax.nn.sigmoid(gl)
    pt = lax.dot_general(wt_ref[...], u, (((1,), (1,)), ((), ())), preferred_element_type=F32)
    rg = slice(3 * D_MLSTM, 3 * D_MLSTM + 4 * N_HEADS)
    for lb in range(tm // LANES):
        cols = slice(lb * LANES, (lb + 1) * LANES)
        g_ref[0, :, cols] = pt[rg, cols] + bt_ref[rg, :]
    scale = HEAD_DIM ** -0.5
    for h in range(N_HEADS):
        for lb in range(tm // LANES):
            c, off = divmod(lb * LANES, chunk)
            cols = slice(lb * LANES, (lb + 1) * LANES)
            dst = (0, h, c, slice(None), slice(off, off + LANES))
            rq = slice(h * HEAD_DIM, (h + 1) * HEAD_DIM)
            rv = slice(D_MLSTM + h * HEAD_DIM, D_MLSTM + (h + 1) * HEAD_DIM)
            ro = slice(2 * D_MLSTM + h * HEAD_DIM, 2 * D_MLSTM + (h + 1) * HEAD_DIM)
            qt_ref[dst] = ((pt[rq, cols] + bt_ref[rq, :]) * scale).astype(BF16)
            vt_ref[dst] = (pt[rv, cols] + bt_ref[rv, :]).astype(BF16)
            ot_ref[dst] = jax.nn.sigmoid(pt[ro, cols] + bt_ref[ro, :]) * gain_ref[rq, :]


def _proj_call(x, mods, ng, wn, bn, wt, bt, gain, *, mod_row, tm, chunk):
    b, t, d = x.shape
    nch, cpt = t // chunk, tm // chunk
    row = (lambda i, j: (i, 0, 0, 0)) if mod_row is None else (lambda i, j: (mod_row, 0, 0, 0))
    tspec = pl.BlockSpec((1, N_HEADS, cpt, HEAD_DIM, chunk), lambda i, j: (i, 0, j, 0, 0))
    tshape = (b, N_HEADS, nch, HEAD_DIM, chunk)
    return pl.pallas_call(
        functools.partial(_proj_kernel, chunk=chunk),
        grid=(b, t // tm),
        in_specs=[pl.BlockSpec((1, tm, d), lambda i, j: (i, j, 0)),
                  pl.BlockSpec((1, N_MOD, 1, d), row),
                  _resident((1, d)), _resident(wn.shape), _resident(bn.shape),
                  _resident(wt.shape), _resident(bt.shape), _resident(gain.shape)],
        out_specs=[pl.BlockSpec((1, N_HEADS, tm, HEAD_DIM), lambda i, j: (i, 0, j, 0)),
                   pl.BlockSpec((1, tm, D_CONV), lambda i, j: (i, j, 0)),
                   pl.BlockSpec((1, 4 * N_HEADS, tm), lambda i, j: (i, 0, j)),
                   tspec, tspec, tspec],
        out_shape=[jax.ShapeDtypeStruct((b, N_HEADS, t, HEAD_DIM), BF16),
                   jax.ShapeDtypeStruct((b, t, D_CONV), F32),
                   jax.ShapeDtypeStruct((b, 4 * N_HEADS, t), F32),
                   jax.ShapeDtypeStruct(tshape, BF16),
                   jax.ShapeDtypeStruct(tshape, BF16),
                   jax.ShapeDtypeStruct(tshape, F32)],
        compiler_params=_cparams(2),
        name="proj",
    )(x, mods, ng.reshape(1, d), wn, bn, wt, bt, gain)


def _mlstm_state_pass(g_ref, kc_ref, vtc_ref, kl_ref, vtl_ref, hh, scratch, masks, orders):
    gs_ref, rt_ref, upd_ref, st_ref = scratch
    _, n_ch, L = g_ref.shape[2:]
    for d in range(2):
        li, fpre = g_ref[0, hh, 2 * d], g_ref[0, hh, 2 * d + 1]
        lf = jnp.minimum(fpre, 0.0) - jnp.log1p(jnp.exp(-jnp.abs(fpre)))
        tri = masks[d].astype(F32)
        cum = jnp.dot(lf, tri, preferred_element_type=F32, precision=lax.Precision.HIGHEST)
        edge = L - 1 if d == 0 else 0
        total = cum[:, edge:edge + 1]
        r = li - cum
        yield
        pm = r
        lane = lax.broadcasted_iota(jnp.int32, r.shape, 1)
        s = 1
        while s < L:
            if d == 1:
                sh = jnp.where(lane < L - s, pltpu.roll(pm, L - s, 1), NEG_BIG)
            else:
                sh = jnp.where(lane >= s, pltpu.roll(pm, s, 1), NEG_BIG)
            pm = jnp.maximum(pm, sh)
            s *= 2
            yield
        rmax = pm[:, edge:edge + 1]
        m = jnp.zeros((1, 1), F32)
        m_prev, m_new = [None] * n_ch, [None] * n_ch
        for c in orders[d]:
            m_prev[c] = m
            m = total[c:c + 1] + jnp.maximum(m, rmax[c:c + 1])
            m_new[c] = m
        yield
        m_prev = jnp.concatenate(m_prev, axis=0)
        m_new = jnp.concatenate(m_new, axis=0)
        mx = jnp.maximum(m_prev, pm)
        alpha = -LOG2E * mx
        inter = jnp.exp(m_prev - mx)
        clampv = jnp.exp(-(cum + mx))
        wk = jnp.exp(total + r - m_new)
        decay = jnp.broadcast_to(jnp.exp(total + m_prev - m_new), (n_ch, L))
        rt_ref[d] = jnp.concatenate([LOG2E * r, jnp.zeros((LANES - n_ch, L), F32)], axis=0).T
        pad = jnp.zeros((3, L), F32)
        for c in range(n_ch):
            gs_ref[d, c] = jnp.concatenate(
                [alpha[c:c + 1], inter[c:c + 1], clampv[c:c + 1], wk[c:c + 1], decay[c:c + 1], pad],
                axis=0)
        yield

    n_c = kc_ref.shape[2] // L
    for c in range(n_ch):
        if c < n_c:
            k, vt = kc_ref[0, hh, c * L:(c + 1) * L, :], vtc_ref[0, hh, c]
        else:
            k, vt = kl_ref[0, hh, (c - n_c) * L:(c - n_c + 1) * L, :], vtl_ref[0, hh, c - n_c]
        vtf = vt.astype(F32)
        parts = []
        for d in range(2):
            wk = gs_ref[d, c][3:4]
            parts += [vtf * wk, wk, jnp.zeros((STATE_ROWS - HEAD_DIM - 1, L), F32)]
        lhs = jnp.concatenate(parts, axis=0).astype(BF16)
        upd = jnp.dot(lhs, k, preferred_element_type=F32)
        upd_ref[0, c] = upd[:STATE_ROWS]
        upd_ref[1, c] = upd[STATE_ROWS:]
        yield

    states = [jnp.zeros((STATE_ROWS, HEAD_DIM), F32)] * 2
    for i in range(n_ch):
        for d in range(2):
            c = orders[d][i]
            st_ref[d, c] = states[d].astype(BF16)
            states[d] = gs_ref[d, c][4:5, :HEAD_DIM] * states[d] + upd_ref[d, c]
        yield


def _conv_cols_pass(z_ref, w_ref, o_ref, zp_ref):
    seg, pitch = GRID_W, GRID_W + 2 * SEG_LPAD
    n_seg = z_ref.shape[1] // seg
    zp_ref[...] = jnp.zeros_like(zp_ref)
    yield
    for s in range(n_seg):
        zp_ref[s * pitch + SEG_LPAD:s * pitch + SEG_LPAD + seg, :] = z_ref[0, s * seg:(s + 1) * seg, :]
        if s % 8 == 7:
            yield
    for s in range(n_seg):
        src = s * pitch + SEG_LPAD - CONV_PAD
        acc = w_ref[0:1, :] * zp_ref[src:src + seg, :]
        for j in range(1, CONV_WIDTH):
            acc = acc + w_ref[j:j + 1, :] * zp_ref[src + j:src + j + seg, :]
        o_ref[0, s * seg:(s + 1) * seg, :] = acc
        yield


def _conv_rows_pass(z_ref, w_ref, o_ref):
    n = z_ref.shape[1] // GRID_W
    for i in range(n):
        acc = None
        for j in range(CONV_WIDTH):
            src = i + j - CONV_PAD
            if 0 <= src < n:
                term = w_ref[j:j + 1, :] * z_ref[0, src * GRID_W:(src + 1) * GRID_W, :]
                acc = term if acc is None else acc + term
        o_ref[0, i * GRID_W:(i + 1) * GRID_W, :] = acc
        yield


def _mlstm_kernel(kc_ref, qtc_ref, vtc_ref, otc_ref, kl_ref, qtl_ref, vtl_ref, otl_ref, g_ref,
                  kcn_ref, vtcn_ref, kln_ref, vtln_ref, gn_ref, zcol_ref, zrow_ref, wcol_ref, wrow_ref,
                  yc_ref, yl_ref, ocol_ref, orow_ref, *scratch, ctx_out):
    n_c, n_l = qtc_ref.shape[2], qtl_ref.shape[2]
    n_ch = n_c + n_l
    L = qtl_ref.shape[4]
    srow = lax.broadcasted_iota(jnp.int32, (L, L), 0)
    tcol = lax.broadcasted_iota(jnp.int32, (L, L), 1)
    masks = (srow <= tcol, srow >= tcol)
    orders = (list(range(n_ch)),
              list(range(n_c - 1, -1, -1)) + list(range(n_ch - 1, n_c - 1, -1)))
    sets = (scratch[:4], scratch[4:8])
    zp_ref = scratch[8]
    cur_refs = (g_ref, kc_ref, vtc_ref, kl_ref, vtl_ref)
    next_refs = (gn_ref, kcn_ref, vtcn_ref, kln_ref, vtln_ref)

    @pl.when((pl.program_id(0) == 0) & (pl.program_id(1) == 0))
    def _():
        for _ in _mlstm_state_pass(*cur_refs, 0, sets[0], masks, orders):
            pass

    for hh in range(2):
        if hh == 0:
            ahead = _mlstm_state_pass(*cur_refs, 1, sets[1], masks, orders)
            conv = _conv_cols_pass(zcol_ref, wcol_ref, ocol_ref, zp_ref)
        else:
            ahead = _mlstm_state_pass(*next_refs, 0, sets[0], masks, orders)
            conv = _conv_rows_pass(zrow_ref, wrow_ref, orow_ref)
        _mlstm_outputs(kc_ref, qtc_ref, vtc_ref, otc_ref, kl_ref, qtl_ref, vtl_ref, otl_ref,
                       yc_ref, yl_ref, hh, sets[hh], masks, (ahead, conv), ctx_out)
        for gen in (ahead, conv):
            for _ in gen:
                pass


def _mlstm_outputs(kc_ref, qtc_ref, vtc_ref, otc_ref, kl_ref, qtl_ref, vtl_ref, otl_ref,
                   yc_ref, yl_ref, hh, scratch, masks, fillers, ctx_out):
    gs_ref, rt_ref, _, st_ref = scratch
    n_c, n_l = qtc_ref.shape[2], qtl_ref.shape[2]
    L = qtl_ref.shape[4]
    lanes = slice(hh * HEAD_DIM, (hh + 1) * HEAD_DIM)
    ahead, conv = fillers

    def tick(boundary=False):
        next(ahead, None)
        if boundary:
            for _ in range(CONV_STAGES_PER_CHUNK):
                next(conv, None)

    def outputs(cidx, k, qt, vt, og):
        s_t = jnp.dot(k, qt, preferred_element_type=F32)
        carried = jnp.dot(jnp.concatenate([st_ref[0, cidx], st_ref[1, cidx]], axis=0), qt,
                          preferred_element_type=F32)
        tick()
        sws, dens, rows = [], [], []
        for d in range(2):
            gsl = gs_ref[d, cidx]
            rows.append(gsl)
            arg = jnp.where(masks[d], rt_ref[d, :, cidx:cidx + 1] + gsl[0:1], NEG_BIG)
            sw = s_t * jnp.exp2(arg)
            sws.append(sw.astype(BF16))
            qn = carried[d * STATE_ROWS + HEAD_DIM:d * STATE_ROWS + HEAD_DIM + 1]
            dens.append(jnp.sum(sw, axis=0, keepdims=True) + gsl[1:2] * qn)
            tick()
        num2 = jnp.dot(vt, jnp.concatenate(sws, axis=1), preferred_element_type=F32)
        tick()
        hsum = None
        for d in range(2):
            num = (num2[:, d * L:(d + 1) * L]
                   + rows[d][1:2] * carried[d * STATE_ROWS:d * STATE_ROWS + HEAD_DIM])
            h = num * (1.0 / jnp.maximum(jnp.abs(dens[d]), rows[d][2:3]))
            hsum = h if hsum is None else hsum + h
        ms = jnp.mean(hsum * hsum, axis=0, keepdims=True)
        y = hsum * lax.rsqrt(ms + EPS) * og
        return y.T.astype(BF16)

    if ctx_out:
        for c in range(n_c):
            yc_ref[0, c * L:(c + 1) * L, lanes] = outputs(
                c, kc_ref[0, hh, c * L:(c + 1) * L, :], qtc_ref[0, hh, c], vtc_ref[0, hh, c],
                otc_ref[0, hh, c])
            tick(boundary=True)
    else:
        yc_ref[0, :, lanes] = jnp.zeros((yc_ref.shape[1], HEAD_DIM), yc_ref.dtype)

    for c in range(n_l):
        yl_ref[0, c * L:(c + 1) * L, lanes] = outputs(
            n_c + c, kl_ref[0, hh, c * L:(c + 1) * L, :], qtl_ref[0, hh, c], vtl_ref[0, hh, c],
            otl_ref[0, hh, c])
        tick(boundary=True)


def _mlstm_call(kc, qtc, vtc, otc, kl, qtl, vtl, otl, gates, z, w_conv, *, ctx_out):
    b = kl.shape[0]
    tc, tl = kc.shape[2], kl.shape[2]
    n_c, n_l, L = qtc.shape[2], qtl.shape[2], qtl.shape[4]
    n_ch = n_c + n_l
    assert L == LANES and n_ch <= L

    pairs = N_HEADS // 2

    def nat(t):
        return pl.BlockSpec((1, 2, t, HEAD_DIM), lambda i, j: (i, j, 0, 0))

    def tr(n):
        return pl.BlockSpec((1, 2, n, HEAD_DIM, L), lambda i, j: (i, j, 0, 0, 0))

    def nxt(*tail):
        def index(i, j):
            f = jnp.minimum(i * pairs + j + 1, b * pairs - 1)
            return (f // pairs, 2 * (f % pairs)) + tail
        return index

    def nat_next(t):
        return pl.BlockSpec((1, 1, t, HEAD_DIM), nxt(0, 0))

    def tr_next(n):
        return pl.BlockSpec((1, 1, n, HEAD_DIM, L), nxt(0, 0, 0))

    scratch_set = [pltpu.VMEM((2, n_ch, 8, L), F32),
                   pltpu.VMEM((2, L, LANES), F32),
                   pltpu.VMEM((2, n_ch, STATE_ROWS, HEAD_DIM), F32),
                   pltpu.VMEM((2, n_ch, STATE_ROWS, HEAD_DIM), BF16)]
    assert D_CONV == 2 * pairs * LANES and tl % GRID_W == 0
    zblk = (1, tl, LANES)
    half = D_CONV // 2
    return pl.pallas_call(
        functools.partial(_mlstm_kernel, ctx_out=ctx_out),
        grid=(b, pairs),
        in_specs=[nat(tc), tr(n_c), tr(n_c), tr(n_c), nat(tl), tr(n_l), tr(n_l), tr(n_l),
                  pl.BlockSpec((1, 2, 4, n_ch, L), lambda i, j: (i, j, 0, 0, 0)),
                  nat_next(tc), tr_next(n_c), nat_next(tl), tr_next(n_l),
                  pl.BlockSpec((1, 1, 4, n_ch, L), nxt(0, 0, 0)),
                  pl.BlockSpec(zblk, lambda i, j: (i, 0, j)),
                  pl.BlockSpec(zblk, lambda i, j: (i, 0, pairs + j)),
                  pl.BlockSpec((CONV_WIDTH, LANES), lambda i, j: (0, j)),
                  pl.BlockSpec((CONV_WIDTH, LANES), lambda i, j: (0, pairs + j))],
        out_specs=[pl.BlockSpec((1, tc, 2 * HEAD_DIM), lambda i, j: (i, 0, j)),
                   pl.BlockSpec((1, tl, 2 * HEAD_DIM), lambda i, j: (i, 0, j)),
                   pl.BlockSpec(zblk, lambda i, j: (i, 0, j)),
                   pl.BlockSpec(zblk, lambda i, j: (i, 0, j))],
        out_shape=[jax.ShapeDtypeStruct((b, tc, D_MLSTM), BF16),
                   jax.ShapeDtypeStruct((b, tl, D_MLSTM), BF16),
                   jax.ShapeDtypeStruct((b, tl, half), F32),
                   jax.ShapeDtypeStruct((b, tl, half), F32)],
        scratch_shapes=scratch_set + scratch_set
                       + [pltpu.VMEM(((tl // GRID_W) * (GRID_W + 2 * SEG_LPAD), LANES), F32)],
        compiler_params=_cparams(2),
        name="mlstm",
    )(kc, qtc, vtc, otc, kl, qtl, vtl, otl, gates, kc, vtc, kl, vtl, gates, z, z, w_conv, w_conv)


def _conv_kernel(z_ref, w_ref, o_ref, zp_ref, *, seg):
    t, ch = z_ref.shape[1], z_ref.shape[2]
    pitch = seg + 2 * SEG_LPAD
    n_seg = t // seg
    tiles = seg // ROW_TILE
    for blk in range(ch // LANES):
        cols = slice(blk * LANES, (blk + 1) * LANES)
        zp_ref[...] = jnp.zeros_like(zp_ref)
        for s in range(n_seg):
            zp_ref[s * pitch + SEG_LPAD:s * pitch + SEG_LPAD + seg, :] = z_ref[0, s * seg:(s + 1) * seg, cols]
        wrows = [w_ref[j:j + 1, cols] for j in range(CONV_WIDTH)]

        def tile(i, carry):
            s, q = i // tiles, i % tiles
            src = s * pitch + q * ROW_TILE + (SEG_LPAD - CONV_PAD)
            acc = wrows[0] * zp_ref[pl.ds(src, ROW_TILE), :]
            for j in range(1, CONV_WIDTH):
                acc = acc + wrows[j] * zp_ref[pl.ds(src + j, ROW_TILE), :]
            o_ref[0, pl.ds(pl.multiple_of(i * ROW_TILE, ROW_TILE), ROW_TILE), cols] = acc
            return carry

        lax.fori_loop(0, t // ROW_TILE, tile, 0)


def _conv_call(z, w, seg):
    b, t, ch = z.shape
    assert t % seg == 0 and seg % ROW_TILE == 0 and ch % LANES == 0
    return pl.pallas_call(
        functools.partial(_conv_kernel, seg=seg),
        grid=(b,),
        in_specs=[pl.BlockSpec((1, t, ch), lambda i: (i, 0, 0)),
                  pl.BlockSpec((CONV_WIDTH, ch), lambda i: (0, 0))],
        out_specs=pl.BlockSpec((1, t, ch), lambda i: (i, 0, 0)),
        out_shape=jax.ShapeDtypeStruct((b, t, ch), F32),
        scratch_shapes=[pltpu.VMEM(((t // seg) * (seg + 2 * SEG_LPAD), LANES), F32)],
        compiler_params=_cparams(1),
        name="conv",
    )(z, w)


def _gate_rows(g, chunk):
    b, _, t = g.shape
    return g.reshape(b, 4, N_HEADS, t // chunk, chunk).transpose(0, 2, 1, 3, 4)


def kernel(x, c, ctx, c_ctx, w_mod, b_mod, norm_ffn1, ffn1_up, ffn1_down, norm_mix, w_in, b_in,
           mlstm_norm, conv_w, conv_b, conv_norm_g, conv_norm_b, w_out, norm_ffn2, ffn2_up,
           ffn2_down, norm_final):
    b, s, d = x.shape
    depth = w_mod.shape[0]
    d_ff = ffn1_down.shape[1]
    nck = d_ff // FF_CHUNK
    tm_c = ctx.shape[1]
    L = MLSTM_CHUNK

    def flat(a):
        return a.reshape(1, -1, a.shape[-1])

    rows = -(-(b + 1) // 8) * 8
    c_all = jnp.concatenate([c, c_ctx[None, :], jnp.zeros((rows - b - 1, d), F32)], axis=0)
    mods_all = _mod_call(c_all, w_mod, b_mod).reshape(depth, rows, N_MOD, 1, d)

    q0, k0, v0 = 0, D_MLSTM, 2 * D_MLSTM
    u0 = 4 * D_MLSTM + 4 * N_HEADS

    xc = ctx
    for l in range(depth):
        last = l == depth - 1
        mods = mods_all[l]

        def ffn_weights(up, down):
            return up.astype(BF16), down.reshape(nck, FF_CHUNK, d).astype(BF16)

        w1 = ffn_weights(ffn1_up[l], ffn1_down[l])
        x = _ffn_call(x, mods, norm_ffn1[l], *w1, js=0, mod_row=None, tm=FFN_TILE)
        xc = _ffn_call(flat(xc), mods, norm_ffn1[l], *w1, js=0, mod_row=b, tm=FFN_TILE).reshape(xc.shape)

        wi, bi = w_in[l], b_in[l]
        wn = jnp.concatenate([wi[:, k0:v0], wi[:, u0:]], axis=1).astype(BF16)
        bn = jnp.concatenate([bi[k0:v0], bi[u0:]]).reshape(1, -1)
        wt = jnp.concatenate([wi[:, q0:k0], wi[:, v0:u0]], axis=1).T.astype(BF16)
        bt = jnp.broadcast_to(jnp.concatenate([bi[q0:k0], bi[v0:u0]])[:, None],
                              (wt.shape[0], LANES))
        gain = jnp.broadcast_to(mlstm_norm[l][:, None], (D_MLSTM, LANES))
        kl, zl, gl, qtl, vtl, otl = _proj_call(x, mods, norm_mix[l], wn, bn, wt, bt, gain,
                                               mod_row=None, tm=PROJ_TILE, chunk=L)
        kc, zc, gc, qtc, vtc, otc = _proj_call(xc, mods, norm_mix[l], wn, bn, wt, bt, gain,
                                               mod_row=b, tm=tm_c, chunk=L)

        gates = jnp.concatenate([_gate_rows(gc, L), _gate_rows(gl, L)], axis=3)
        ymc, yml, z_cols, z_rows = _mlstm_call(kc, qtc, vtc, otc, kl, qtl, vtl, otl, gates,
                                               zl, conv_w[l], ctx_out=not last)
        wm = w_out[l][:D_MLSTM].astype(BF16)
        wc = w_out[l][D_MLSTM:].astype(BF16)
        mix = (conv_b[l], conv_norm_g[l], conv_norm_b[l], wm, wc)

        w2 = ffn_weights(ffn2_up[l], ffn2_down[l])
        x = _ffn_call(x, mods, norm_ffn2[l], *w2, js=6, mod_row=None, tm=FFN_TILE,
                      final_g=norm_final if last else None, mix=(yml, [z_cols, z_rows], *mix))
        if not last:
            z_ctx = _conv_call(zc, conv_w[l], zc.shape[1])
            xc = _ffn_call(flat(xc), mods, norm_ffn2[l], *w2, js=6, mod_row=b, tm=FFN_TILE,
                           mix=(flat(ymc), [flat(z_ctx)], *mix)).reshape(xc.shape)
    return x
```

```python
import functools

import jax
import jax.numpy as jnp
from jax import lax
from jax.experimental import pallas as pl
from jax.experimental.pallas import tpu as pltpu

F32 = jnp.float32
BF16 = jnp.bfloat16

GRID_W = 64
N_HEADS = 4
HEAD_DIM = 128
D_MLSTM = N_HEADS * HEAD_DIM
D_CONV = 512
CONV_WIDTH = 31
CONV_PAD = (CONV_WIDTH - 1) // 2
N_MOD = 9
EPS = 1e-6

LANES = 128
FF_CHUNK = 256
FFN_TILE = 1024
PROJ_TILE = 1024
PROLOGUE_SLICES = 4
FFN_UNROLL = 10
MLSTM_CHUNK = 128
STATE_ROWS = HEAD_DIM + 16
CONV_STAGES_PER_CHUNK = 3
SEG_LPAD = 16
ROW_TILE = 64
NEG_BIG = -1e30
LOG2E = 1.4426950408889634
VMEM_LIMIT = 56 * 1024 * 1024


def _cparams(n_axes):
    return pltpu.CompilerParams(
        dimension_semantics=("arbitrary",) * n_axes, vmem_limit_bytes=VMEM_LIMIT)


def _resident(shape):
    nd = len(shape)
    return pl.BlockSpec(shape, lambda *_: (0,) * nd, pipeline_mode=pl.Buffered(1))


def _rms_mod(x, g, shift, scale):
    ms = jnp.mean(x * x, axis=-1, keepdims=True)
    return (x * lax.rsqrt(ms + EPS)) * (g * (1.0 + scale)) + shift


def _mod_kernel(c_ref, w_ref, b_ref, o_ref):
    c = c_ref[...]
    s = (c * jax.nn.sigmoid(c)).astype(BF16)
    o_ref[0] = jnp.dot(s, w_ref[0].astype(BF16), preferred_element_type=F32) + b_ref[0]


def _mod_call(c_all, w_mod, b_mod):
    depth, d, n = w_mod.shape
    r = c_all.shape[0]
    tn = n // 8
    return pl.pallas_call(
        _mod_kernel,
        grid=(depth, n // tn),
        in_specs=[pl.BlockSpec((r, d), lambda l, j: (0, 0)),
                  pl.BlockSpec((1, d, tn), lambda l, j: (l, 0, j)),
                  pl.BlockSpec((1, 1, tn), lambda l, j: (l, 0, j))],
        out_specs=pl.BlockSpec((1, r, tn), lambda l, j: (l, 0, j)),
        out_shape=jax.ShapeDtypeStruct((depth, r, n), F32),
        compiler_params=_cparams(2),
        name="mod",
    )(c_all, w_mod, b_mod.reshape(depth, 1, n))


def _ffn_kernel(*refs, js, final, nz):
    x_ref, mod_ref = refs[:2]
    refs = refs[2:]
    if nz:
        ym_ref, z_refs = refs[0], refs[1:1 + nz]
        cb_ref, cg_ref, cnb_ref, wm_ref, wc_ref = refs[1 + nz:6 + nz]
        refs = refs[6 + nz:]
    ng_ref, wup_ref, wd_ref = refs[:3]
    refs = refs[3:]
    if final:
        gf_ref, refs = refs[0], refs[1:]
    o_ref, h_ref, a_ref = refs
    nck, fc, _ = wd_ref.shape
    d_ff = nck * fc

    def up(c, rows=slice(None)):
        h = h_ref[rows, :]
        col = c * fc if isinstance(c, int) else pl.multiple_of(c * fc, fc)
        g = jnp.dot(h, wup_ref[:, pl.ds(col, fc)], preferred_element_type=F32)
        u = jnp.dot(h, wup_ref[:, pl.ds(d_ff + col, fc)], preferred_element_type=F32)
        return (g * jax.nn.sigmoid(g) * u).astype(BF16)

    tm = x_ref.shape[1]
    n_slices = 1 if nz else PROLOGUE_SLICES
    for r in range(n_slices):
        rows = slice(r * (tm // n_slices), (r + 1) * (tm // n_slices))
        x = x_ref[0, rows, :]
        if nz:
            zs = [zr[0, rows, :] for zr in z_refs]
            z = (jnp.concatenate(zs, axis=-1) if nz > 1 else zs[0]) + cb_ref[...]
            zc = z - jnp.mean(z, axis=-1, keepdims=True)
            var = jnp.mean(zc * zc, axis=-1, keepdims=True)
            ln = zc * lax.rsqrt(var + EPS) * cg_ref[...] + cnb_ref[...]
            yc = (ln * jax.nn.sigmoid(ln)).astype(BF16)
            y = (jnp.dot(ym_ref[0, rows, :], wm_ref[...], preferred_element_type=F32)
                 + jnp.dot(yc, wc_ref[...], preferred_element_type=F32))
            x = x + mod_ref[0, 5] * y
        o_ref[0, rows, :] = x
        h_ref[rows, :] = _rms_mod(x, ng_ref[...], mod_ref[0, js], mod_ref[0, js + 1]).astype(BF16)
        a_ref[0, rows, :] = up(0, rows)

    def chunk(c, acc):
        a_prev = a_ref[(c - 1) % 2]
        a_ref[c % 2] = up(c)
        return acc + jnp.dot(a_prev, wd_ref[c - 1], preferred_element_type=F32)

    acc = lax.fori_loop(1, nck, chunk, jnp.zeros((tm, o_ref.shape[2]), F32), unroll=FFN_UNROLL)
    d_last = jnp.dot(a_ref[(nck - 1) % 2], wd_ref[nck - 1], preferred_element_type=F32)
    out = o_ref[0] + (0.5 * mod_ref[0, js + 2]) * (acc + d_last)
    if final:
        ms = jnp.mean(out * out, axis=-1, keepdims=True)
        out = out * lax.rsqrt(ms + EPS) * gf_ref[...]
    o_ref[0] = out


def _ffn_call(x, mods, ng, wup, wd, *, js, mod_row, tm, final_g=None, mix=None):
    b, t, d = x.shape
    fc = wd.shape[1]
    row = (lambda i, j: (i, 0, 0, 0)) if mod_row is None else (lambda i, j: (mod_row, 0, 0, 0))

    def tok(c):
        return pl.BlockSpec((1, tm, c), lambda i, j: (i, j, 0))

    in_specs = [tok(d), pl.BlockSpec((1, N_MOD, 1, d), row)]
    args = [x, mods]
    nz = 0
    if mix is not None:
        ym, zs, cb, cg, cnb, wm, wc = mix
        nz = len(zs)
        in_specs += [tok(ym.shape[2])] + [tok(z.shape[2]) for z in zs]
        in_specs += [_resident((1, D_CONV))] * 3 + [_resident(wm.shape), _resident(wc.shape)]
        args += [ym, *zs, cb.reshape(1, -1), cg.reshape(1, -1), cnb.reshape(1, -1), wm, wc]
    in_specs += [_resident((1, d)), _resident(wup.shape), _resident(wd.shape)]
    args += [ng.reshape(1, d), wup, wd]
    if final_g is not None:
        in_specs.append(_resident((1, d)))
        args.append(final_g.reshape(1, d))
    return pl.pallas_call(
        functools.partial(_ffn_kernel, js=js, final=final_g is not None, nz=nz),
        grid=(b, t // tm),
        in_specs=in_specs,
        out_specs=tok(d),
        out_shape=jax.ShapeDtypeStruct((b, t, d), F32),
        scratch_shapes=[pltpu.VMEM((tm, d), BF16), pltpu.VMEM((2, tm, fc), BF16)],
        compiler_params=_cparams(2),
        name="ffn",
    )(*args)


def _proj_kernel(x_ref, mod_ref, ng_ref, wn_ref, bn_ref, wt_ref, bt_ref, gain_ref,
                 k_ref, z_ref, g_ref, qt_ref, vt_ref, ot_ref, *, chunk):
    tm = x_ref.shape[1]
    u = _rms_mod(x_ref[0], ng_ref[...], mod_ref[0, 3], mod_ref[0, 4]).astype(BF16)
    pn = jnp.dot(u, wn_ref[...], preferred_element_type=F32) + bn_ref[...]
    for h in range(N_HEADS):
        k_ref[0, h] = pn[:, h * HEAD_DIM:(h + 1) * HEAD_DIM].astype(BF16)
    a = pn[:, D_MLSTM:D_MLSTM + D_CONV]
    gl = pn[:, D_MLSTM + D_CONV:D_MLSTM + 2 * D_CONV]
    z_ref[0] = a * jax.nn.sigmoid(gl)
    pt = lax.dot_general(wt_ref[...], u, (((1,), (1,)), ((), ())), preferred_element_type=F32)
    rg = slice(3 * D_MLSTM, 3 * D_MLSTM + 4 * N_HEADS)
    for lb in range(tm // LANES):
        cols = slice(lb * LANES, (lb + 1) * LANES)
        g_ref[0, :, cols] = pt[rg, cols] + bt_ref[rg, :]
    scale = HEAD_DIM ** -0.5
    for h in range(N_HEADS):
        for lb in range(tm // LANES):
            c, off = divmod(lb * LANES, chunk)
            cols = slice(lb * LANES, (lb + 1) * LANES)
            dst = (0, h, c, slice(None), slice(off, off + LANES))
            rq = slice(h * HEAD_DIM, (h + 1) * HEAD_DIM)
            rv = slice(D_MLSTM + h * HEAD_DIM, D_MLSTM + (h + 1) * HEAD_DIM)
            ro = slice(2 * D_MLSTM + h * HEAD_DIM, 2 * D_MLSTM + (h + 1) * HEAD_DIM)
            qt_ref[dst] = ((pt[rq, cols] + bt_ref[rq, :]) * scale).astype(BF16)
            vt_ref[dst] = (pt[rv, cols] + bt_ref[rv, :]).astype(BF16)
            ot_ref[dst] = jax.nn.sigmoid(pt[ro, cols] + bt_ref[ro, :]) * gain_ref[rq, :]


def _proj_call(x, mods, ng, wn, bn, wt, bt, gain, *, mod_row, tm, chunk):
    b, t, d = x.shape
    nch, cpt = t // chunk, tm // chunk
    row = (lambda i, j: (i, 0, 0, 0)) if mod_row is None else (lambda i, j: (mod_row, 0, 0, 0))
    tspec = pl.BlockSpec((1, N_HEADS, cpt, HEAD_DIM, chunk), lambda i, j: (i, 0, j, 0, 0))
    tshape = (b, N_HEADS, nch, HEAD_DIM, chunk)
    return pl.pallas_call(
        functools.partial(_proj_kernel, chunk=chunk),
        grid=(b, t // tm),
        in_specs=[pl.BlockSpec((1, tm, d), lambda i, j: (i, j, 0)),
                  pl.BlockSpec((1, N_MOD, 1, d), row),
                  _resident((1, d)), _resident(wn.shape), _resident(bn.shape),
                  _resident(wt.shape), _resident(bt.shape), _resident(gain.shape)],
        out_specs=[pl.BlockSpec((1, N_HEADS, tm, HEAD_DIM), lambda i, j: (i, 0, j, 0)),
                   pl.BlockSpec((1, tm, D_CONV), lambda i, j: (i, j, 0)),
                   pl.BlockSpec((1, 4 * N_HEADS, tm), lambda i, j: (i, 0, j)),
                   tspec, tspec, tspec],
        out_shape=[jax.ShapeDtypeStruct((b, N_HEADS, t, HEAD_DIM), BF16),
                   jax.ShapeDtypeStruct((b, t, D_CONV), F32),
                   jax.ShapeDtypeStruct((b, 4 * N_HEADS, t), F32),
                   jax.ShapeDtypeStruct(tshape, BF16),
                   jax.ShapeDtypeStruct(tshape, BF16),
                   jax.ShapeDtypeStruct(tshape, F32)],
        compiler_params=_cparams(2),
        name="proj",
    )(x, mods, ng.reshape(1, d), wn, bn, wt, bt, gain)


def _mlstm_state_pass(g_ref, kc_ref, vtc_ref, kl_ref, vtl_ref, hh, scratch, masks, orders):
    gs_ref, rt_ref, upd_ref, st_ref = scratch
    _, n_ch, L = g_ref.shape[2:]
    for d in range(2):
        li, fpre = g_ref[0, hh, 2 * d], g_ref[0, hh, 2 * d + 1]
        lf = jnp.minimum(fpre, 0.0) - jnp.log1p(jnp.exp(-jnp.abs(fpre)))
        tri = masks[d].astype(F32)
        cum = jnp.dot(lf, tri, preferred_element_type=F32, precision=lax.Precision.HIGHEST)
        edge = L - 1 if d == 0 else 0
        total = cum[:, edge:edge + 1]
        r = li - cum
        yield
        pm = r
        lane = lax.broadcasted_iota(jnp.int32, r.shape, 1)
        s = 1
        while s < L:
            if d == 1:
                sh = jnp.where(lane < L - s, pltpu.roll(pm, L - s, 1), NEG_BIG)
            else:
                sh = jnp.where(lane >= s, pltpu.roll(pm, s, 1), NEG_BIG)
            pm = jnp.maximum(pm, sh)
            s *= 2
            yield
        rmax = pm[:, edge:edge + 1]
        m = jnp.zeros((1, 1), F32)
        m_prev, m_new = [None] * n_ch, [None] * n_ch
        for c in orders[d]:
            m_prev[c] = m
            m = total[c:c + 1] + jnp.maximum(m, rmax[c:c + 1])
            m_new[c] = m
        yield
        m_prev = jnp.concatenate(m_prev, axis=0)
        m_new = jnp.concatenate(m_new, axis=0)
        mx = jnp.maximum(m_prev, pm)
        alpha = -LOG2E * mx
        inter = jnp.exp(m_prev - mx)
        clampv = jnp.exp(-(cum + mx))
        wk = jnp.exp(total + r - m_new)
        decay = jnp.broadcast_to(jnp.exp(total + m_prev - m_new), (n_ch, L))
        rt_ref[d] = jnp.concatenate([LOG2E * r, jnp.zeros((LANES - n_ch, L), F32)], axis=0).T
        pad = jnp.zeros((3, L), F32)
        for c in range(n_ch):
            gs_ref[d, c] = jnp.concatenate(
                [alpha[c:c + 1], inter[c:c + 1], clampv[c:c + 1], wk[c:c + 1], decay[c:c + 1], pad],
                axis=0)
        yield

    n_c = kc_ref.shape[2] // L
    for c in range(n_ch):
        if c < n_c:
            k, vt = kc_ref[0, hh, c * L:(c + 1) * L, :], vtc_ref[0, hh, c]
        else:
            k, vt = kl_ref[0, hh, (c - n_c) * L:(c - n_c + 1) * L, :], vtl_ref[0, hh, c - n_c]
        vtf = vt.astype(F32)
        parts = []
        for d in range(2):
            wk = gs_ref[d, c][3:4]
            parts += [vtf * wk, wk, jnp.zeros((STATE_ROWS - HEAD_DIM - 1, L), F32)]
        lhs = jnp.concatenate(parts, axis=0).astype(BF16)
        upd = jnp.dot(lhs, k, preferred_element_type=F32)
        upd_ref[0, c] = upd[:STATE_ROWS]
        upd_ref[1, c] = upd[STATE_ROWS:]
        yield

    states = [jnp.zeros((STATE_ROWS, HEAD_DIM), F32)] * 2
    for i in range(n_ch):
        for d in range(2):
            c = orders[d][i]
            st_ref[d, c] = states[d].astype(BF16)
            states[d] = gs_ref[d, c][4:5, :HEAD_DIM] * states[d] + upd_ref[d, c]
        yield


def _conv_cols_pass(z_ref, w_ref, o_ref, zp_ref):
    seg, pitch = GRID_W, GRID_W + 2 * SEG_LPAD
    n_seg = z_ref.shape[1] // seg
    zp_ref[...] = jnp.zeros_like(zp_ref)
    yield
    for s in range(n_seg):
        zp_ref[s * pitch + SEG_LPAD:s * pitch + SEG_LPAD + seg, :] = z_ref[0, s * seg:(s + 1) * seg, :]
        if s % 8 == 7:
            yield
    for s in range(n_seg):
        src = s * pitch + SEG_LPAD - CONV_PAD
        acc = w_ref[0:1, :] * zp_ref[src:src + seg, :]
        for j in range(1, CONV_WIDTH):
            acc = acc + w_ref[j:j + 1, :] * zp_ref[src + j:src + j + seg, :]
        o_ref[0, s * seg:(s + 1) * seg, :] = acc
        yield


def _conv_rows_pass(z_ref, w_ref, o_ref):
    n = z_ref.shape[1] // GRID_W
    for i in range(n):
        acc = None
        for j in range(CONV_WIDTH):
            src = i + j - CONV_PAD
            if 0 <= src < n:
                term = w_ref[j:j + 1, :] * z_ref[0, src * GRID_W:(src + 1) * GRID_W, :]
                acc = term if acc is None else acc + term
        o_ref[0, i * GRID_W:(i + 1) * GRID_W, :] = acc
        yield


def _mlstm_kernel(kc_ref, qtc_ref, vtc_ref, otc_ref, kl_ref, qtl_ref, vtl_ref, otl_ref, g_ref,
                  kcn_ref, vtcn_ref, kln_ref, vtln_ref, gn_ref, zcol_ref, zrow_ref, wcol_ref, wrow_ref,
                  yc_ref, yl_ref, ocol_ref, orow_ref, *scratch, ctx_out):
    n_c, n_l = qtc_ref.shape[2], qtl_ref.shape[2]
    n_ch = n_c + n_l
    L = qtl_ref.shape[4]
    srow = lax.broadcasted_iota(jnp.int32, (L, L), 0)
    tcol = lax.broadcasted_iota(jnp.int32, (L, L), 1)
    masks = (srow <= tcol, srow >= tcol)
    orders = (list(range(n_ch)),
              list(range(n_c - 1, -1, -1)) + list(range(n_ch - 1, n_c - 1, -1)))
    sets = (scratch[:4], scratch[4:8])
    zp_ref = scratch[8]
    cur_refs = (g_ref, kc_ref, vtc_ref, kl_ref, vtl_ref)
    next_refs = (gn_ref, kcn_ref, vtcn_ref, kln_ref, vtln_ref)

    @pl.when((pl.program_id(0) == 0) & (pl.program_id(1) == 0))
    def _():
        for _ in _mlstm_state_pass(*cur_refs, 0, sets[0], masks, orders):
            pass

    for hh in range(2):
        if hh == 0:
            ahead = _mlstm_state_pass(*cur_refs, 1, sets[1], masks, orders)
            conv = _conv_cols_pass(zcol_ref, wcol_ref, ocol_ref, zp_ref)
        else:
            ahead = _mlstm_state_pass(*next_refs, 0, sets[0], masks, orders)
            conv = _conv_rows_pass(zrow_ref, wrow_ref, orow_ref)
        _mlstm_outputs(kc_ref, qtc_ref, vtc_ref, otc_ref, kl_ref, qtl_ref, vtl_ref, otl_ref,
                       yc_ref, yl_ref, hh, sets[hh], masks, (ahead, conv), ctx_out)
        for gen in (ahead, conv):
            for _ in gen:
                pass


def _mlstm_outputs(kc_ref, qtc_ref, vtc_ref, otc_ref, kl_ref, qtl_ref, vtl_ref, otl_ref,
                   yc_ref, yl_ref, hh, scratch, masks, fillers, ctx_out):
    gs_ref, rt_ref, _, st_ref = scratch
    n_c, n_l = qtc_ref.shape[2], qtl_ref.shape[2]
    L = qtl_ref.shape[4]
    lanes = slice(hh * HEAD_DIM, (hh + 1) * HEAD_DIM)
    ahead, conv = fillers

    def tick(boundary=False):
        next(ahead, None)
        if boundary:
            for _ in range(CONV_STAGES_PER_CHUNK):
                next(conv, None)

    def outputs(cidx, k, qt, vt, og):
        s_t = jnp.dot(k, qt, preferred_element_type=F32)
        carried = jnp.dot(jnp.concatenate([st_ref[0, cidx], st_ref[1, cidx]], axis=0), qt,
                          preferred_element_type=F32)
        tick()
        sws, dens, rows = [], [], []
        for d in range(2):
            gsl = gs_ref[d, cidx]
            rows.append(gsl)
            arg = jnp.where(masks[d], rt_ref[d, :, cidx:cidx + 1] + gsl[0:1], NEG_BIG)
            sw = s_t * jnp.exp2(arg)
            sws.append(sw.astype(BF16))
            qn = carried[d * STATE_ROWS + HEAD_DIM:d * STATE_ROWS + HEAD_DIM + 1]
            dens.append(jnp.sum(sw, axis=0, keepdims=True) + gsl[1:2] * qn)
            tick()
        num2 = jnp.dot(vt, jnp.concatenate(sws, axis=1), preferred_element_type=F32)
        tick()
        hsum = None
        for d in range(2):
            num = (num2[:, d * L:(d + 1) * L]
                   + rows[d][1:2] * carried[d * STATE_ROWS:d * STATE_ROWS + HEAD_DIM])
            h = num * (1.0 / jnp.maximum(jnp.abs(dens[d]), rows[d][2:3]))
            hsum = h if hsum is None else hsum + h
        ms = jnp.mean(hsum * hsum, axis=0, keepdims=True)
        y = hsum * lax.rsqrt(ms + EPS) * og
        return y.T.astype(BF16)

    if ctx_out:
        for c in range(n_c):
            yc_ref[0, c * L:(c + 1) * L, lanes] = outputs(
                c, kc_ref[0, hh, c * L:(c + 1) * L, :], qtc_ref[0, hh, c], vtc_ref[0, hh, c],
                otc_ref[0, hh, c])
            tick(boundary=True)
    else:
        yc_ref[0, :, lanes] = jnp.zeros((yc_ref.shape[1], HEAD_DIM), yc_ref.dtype)

    for c in range(n_l):
        yl_ref[0, c * L:(c + 1) * L, lanes] = outputs(
            n_c + c, kl_ref[0, hh, c * L:(c + 1) * L, :], qtl_ref[0, hh, c], vtl_ref[0, hh, c],
            otl_ref[0, hh, c])
        tick(boundary=True)


def _mlstm_call(kc, qtc, vtc, otc, kl, qtl, vtl, otl, gates, z, w_conv, *, ctx_out):
    b = kl.shape[0]
    tc, tl = kc.shape[2], kl.shape[2]
    n_c, n_l, L = qtc.shape[2], qtl.shape[2], qtl.shape[4]
    n_ch = n_c + n_l
    assert L == LANES and n_ch <= L

    pairs = N_HEADS // 2

    def nat(t):
        return pl.BlockSpec((1, 2, t, HEAD_DIM), lambda i, j: (i, j, 0, 0))

    def tr(n):
        return pl.BlockSpec((1, 2, n, HEAD_DIM, L), lambda i, j: (i, j, 0, 0, 0))

    def nxt(*tail):
        def index(i, j):
            f = jnp.minimum(i * pairs + j + 1, b * pairs - 1)
            return (f // pairs, 2 * (f % pairs)) + tail
        return index

    def nat_next(t):
        return pl.BlockSpec((1, 1, t, HEAD_DIM), nxt(0, 0))

    def tr_next(n):
        return pl.BlockSpec((1, 1, n, HEAD_DIM, L), nxt(0, 0, 0))

    scratch_set = [pltpu.VMEM((2, n_ch, 8, L), F32),
                   pltpu.VMEM((2, L, LANES), F32),
                   pltpu.VMEM((2, n_ch, STATE_ROWS, HEAD_DIM), F32),
                   pltpu.VMEM((2, n_ch, STATE_ROWS, HEAD_DIM), BF16)]
    assert D_CONV == 2 * pairs * LANES and tl % GRID_W == 0
    zblk = (1, tl, LANES)
    half = D_CONV // 2
    return pl.pallas_call(
        functools.partial(_mlstm_kernel, ctx_out=ctx_out),
        grid=(b, pairs),
        in_specs=[nat(tc), tr(n_c), tr(n_c), tr(n_c), nat(tl), tr(n_l), tr(n_l), tr(n_l),
                  pl.BlockSpec((1, 2, 4, n_ch, L), lambda i, j: (i, j, 0, 0, 0)),
                  nat_next(tc), tr_next(n_c), nat_next(tl), tr_next(n_l),
                  pl.BlockSpec((1, 1, 4, n_ch, L), nxt(0, 0, 0)),
                  pl.BlockSpec(zblk, lambda i, j: (i, 0, j)),
                  pl.BlockSpec(zblk, lambda i, j: (i, 0, pairs + j)),
                  pl.BlockSpec((CONV_WIDTH, LANES), lambda i, j: (0, j)),
                  pl.BlockSpec((CONV_WIDTH, LANES), lambda i, j: (0, pairs + j))],
        out_specs=[pl.BlockSpec((1, tc, 2 * HEAD_DIM), lambda i, j: (i, 0, j)),
                   pl.BlockSpec((1, tl, 2 * HEAD_DIM), lambda i, j: (i, 0, j)),
                   pl.BlockSpec(zblk, lambda i, j: (i, 0, j)),
                   pl.BlockSpec(zblk, lambda i, j: (i, 0, j))],
        out_shape=[jax.ShapeDtypeStruct((b, tc, D_MLSTM), BF16),
                   jax.ShapeDtypeStruct((b, tl, D_MLSTM), BF16),
                   jax.ShapeDtypeStruct((b, tl, half), F32),
                   jax.ShapeDtypeStruct((b, tl, half), F32)],
        scratch_shapes=scratch_set + scratch_set
                       + [pltpu.VMEM(((tl // GRID_W) * (GRID_W + 2 * SEG_LPAD), LANES), F32)],
        compiler_params=_cparams(2),
        name="mlstm",
    )(kc, qtc, vtc, otc, kl, qtl, vtl, otl, gates, kc, vtc, kl, vtl, gates, z, z, w_conv, w_conv)


def _conv_kernel(z_ref, w_ref, o_ref, zp_ref, *, seg):
    t, ch = z_ref.shape[1], z_ref.shape[2]
    pitch = seg + 2 * SEG_LPAD
    n_seg = t // seg
    tiles = seg // ROW_TILE
    for blk in range(ch // LANES):
        cols = slice(blk * LANES, (blk + 1) * LANES)
        zp_ref[...] = jnp.zeros_like(zp_ref)
        for s in range(n_seg):
            zp_ref[s * pitch + SEG_LPAD:s * pitch + SEG_LPAD + seg, :] = z_ref[0, s * seg:(s + 1) * seg, cols]
        wrows = [w_ref[j:j + 1, cols] for j in range(CONV_WIDTH)]

        def tile(i, carry):
            s, q = i // tiles, i % tiles
            src = s * pitch + q * ROW_TILE + (SEG_LPAD - CONV_PAD)
            acc = wrows[0] * zp_ref[pl.ds(src, ROW_TILE), :]
            for j in range(1, CONV_WIDTH):
                acc = acc + wrows[j] * zp_ref[pl.ds(src + j, ROW_TILE), :]
            o_ref[0, pl.ds(pl.multiple_of(i * ROW_TILE, ROW_TILE), ROW_TILE), cols] = acc
            return carry

        lax.fori_loop(0, t // ROW_TILE, tile, 0)


def _conv_call(z, w, seg):
    b, t, ch = z.shape
    assert t % seg == 0 and seg % ROW_TILE == 0 and ch % LANES == 0
    return pl.pallas_call(
        functools.partial(_conv_kernel, seg=seg),
        grid=(b,),
        in_specs=[pl.BlockSpec((1, t, ch), lambda i: (i, 0, 0)),
                  pl.BlockSpec((CONV_WIDTH, ch), lambda i: (0, 0))],
        out_specs=pl.BlockSpec((1, t, ch), lambda i: (i, 0, 0)),
        out_shape=jax.ShapeDtypeStruct((b, t, ch), F32),
        scratch_shapes=[pltpu.VMEM(((t // seg) * (seg + 2 * SEG_LPAD), LANES), F32)],
        compiler_params=_cparams(1),
        name="conv",
    )(z, w)


def _gate_rows(g, chunk):
    b, _, t = g.shape
    return g.reshape(b, 4, N_HEADS, t // chunk, chunk).transpose(0, 2, 1, 3, 4)


def kernel(x, c, ctx, c_ctx, w_mod, b_mod, norm_ffn1, ffn1_up, ffn1_down, norm_mix, w_in, b_in,
           mlstm_norm, conv_w, conv_b, conv_norm_g, conv_norm_b, w_out, norm_ffn2, ffn2_up,
           ffn2_down, norm_final):
    b, s, d = x.shape
    depth = w_mod.shape[0]
    d_ff = ffn1_down.shape[1]
    nck = d_ff // FF_CHUNK
    tm_c = ctx.shape[1]
    L = MLSTM_CHUNK

    def flat(a):
        return a.reshape(1, -1, a.shape[-1])

    rows = -(-(b + 1) // 8) * 8
    c_all = jnp.concatenate([c, c_ctx[None, :], jnp.zeros((rows - b - 1, d), F32)], axis=0)
    mods_all = _mod_call(c_all, w_mod, b_mod).reshape(depth, rows, N_MOD, 1, d)

    q0, k0, v0 = 0, D_MLSTM, 2 * D_MLSTM
    u0 = 4 * D_MLSTM + 4 * N_HEADS

    xc = ctx
    for l in range(depth):
        last = l == depth - 1
        mods = mods_all[l]

        def ffn_weights(up, down):
            return up.astype(BF16), down.reshape(nck, FF_CHUNK, d).astype(BF16)

        w1 = ffn_weights(ffn1_up[l], ffn1_down[l])
        x = _ffn_call(x, mods, norm_ffn1[l], *w1, js=0, mod_row=None, tm=FFN_TILE)
        xc = _ffn_call(flat(xc), mods, norm_ffn1[l], *w1, js=0, mod_row=b, tm=FFN_TILE).reshape(xc.shape)

        wi, bi = w_in[l], b_in[l]
        wn = jnp.concatenate([wi[:, k0:v0], wi[:, u0:]], axis=1).astype(BF16)
        bn = jnp.concatenate([bi[k0:v0], bi[u0:]]).reshape(1, -1)
        wt = jnp.concatenate([wi[:, q0:k0], wi[:, v0:u0]], axis=1).T.astype(BF16)
        bt = jnp.broadcast_to(jnp.concatenate([bi[q0:k0], bi[v0:u0]])[:, None],
                              (wt.shape[0], LANES))
        gain = jnp.broadcast_to(mlstm_norm[l][:, None], (D_MLSTM, LANES))
        kl, zl, gl, qtl, vtl, otl = _proj_call(x, mods, norm_mix[l], wn, bn, wt, bt, gain,
                                               mod_row=None, tm=PROJ_TILE, chunk=L)
        kc, zc, gc, qtc, vtc, otc = _proj_call(xc, mods, norm_mix[l], wn, bn, wt, bt, gain,
                                               mod_row=b, tm=tm_c, chunk=L)

        gates = jnp.concatenate([_gate_rows(gc, L), _gate_rows(gl, L)], axis=3)
        ymc, yml, z_cols, z_rows = _mlstm_call(kc, qtc, vtc, otc, kl, qtl, vtl, otl, gates,
                                               zl, conv_w[l], ctx_out=not last)
        wm = w_out[l][:D_MLSTM].astype(BF16)
        wc = w_out[l][D_MLSTM:].astype(BF16)
        mix = (conv_b[l], conv_norm_g[l], conv_norm_b[l], wm, wc)

        w2 = ffn_weights(ffn2_up[l], ffn2_down[l])
        x = _ffn_call(x, mods, norm_ffn2[l], *w2, js=6, mod_row=None, tm=FFN_TILE,
                      final_g=norm_final if last else None, mix=(yml, [z_cols, z_rows], *mix))
        if not last:
            z_ctx = _conv_call(zc, conv_w[l], zc.shape[1])
            xc = _ffn_call(flat(xc), mods, norm_ffn2[l], *w2, js=6, mod_row=b, tm=FFN_TILE,
                           mix=(flat(ymc), [flat(z_ctx)], *mix)).reshape(xc.shape)
    return x
```
